```python
import math
import jax, jax.numpy as jnp
from jax import lax
import numpy as np

D_MODEL = 1024
BATCH = 2
SEQ = 8192
DEPTH = 1
DEC_BATCH = 32
DEC_SEQ = 4
PAST_LEN = 8192
PAGE_SIZE = 128

A_HEADS = 8
A_HD = 64
A_QK = 2 * A_HD
A_DV = 2 * A_HD
ROPE_THETA = 10000.0
Q_BLOCK = 128
B_HEADS = 8
B_DK = 128
B_DV = 128
CONV_W = 4
DN_CHUNK = 64
D_FF = 2816
NORM_EPS = 1e-6

A_Q_WIDTH = A_HEADS * A_QK
A_V_WIDTH = A_HEADS * A_DV
B_K_WIDTH = B_HEADS * B_DK
B_V_WIDTH = B_HEADS * B_DV
CONV_CH = 2 * B_K_WIDTH + B_V_WIDTH
IN_SPLITS = (A_Q_WIDTH, A_Q_WIDTH, A_V_WIDTH, CONV_CH, B_V_WIDTH, B_HEADS, B_HEADS, D_MODEL, D_MODEL)
D_IN = sum(IN_SPLITS)

kernel_name = "hybrid_diffattn_gdn_macaron_step"


def rmsnorm(x, w):
    xf = x.astype(jnp.float32)
    y = xf * lax.rsqrt(jnp.mean(xf * xf, axis=-1, keepdims=True) + NORM_EPS) * w.astype(jnp.float32)
    return y.astype(x.dtype)


def l2norm(x):
    xf = x.astype(jnp.float32)
    return xf * lax.rsqrt(jnp.sum(xf * xf, axis=-1, keepdims=True) + NORM_EPS)


def swiglu(x, w_in, w_out):
    gate, up = jnp.split(x @ w_in, 2, axis=-1)
    return (jax.nn.silu(gate) * up) @ w_out


def rope(x, pos):
    d = x.shape[-1]
    half = d // 2
    inv = jnp.power(ROPE_THETA, -2.0 * jnp.arange(half, dtype=jnp.float32) / d)
    ang = pos.astype(jnp.float32)[:, None] * inv[None, :]
    bshape = (pos.shape[0],) + (1,) * (x.ndim - 3) + (half,)
    cos = jnp.cos(ang).reshape(bshape)
    sin = jnp.sin(ang).reshape(bshape)
    xf = x.astype(jnp.float32)
    x1, x2 = xf[..., :half], xf[..., half:]
    return jnp.concatenate([x1 * cos - x2 * sin, x2 * cos + x1 * sin], axis=-1).astype(x.dtype)


def diff_attn_block(q, k, v, q_pos, k_pos, lam):
    s = jnp.einsum('bqhmd,bkhmd->mbhqk', q, k)
    mask = k_pos[None, :] <= q_pos[:, None]
    p = jax.nn.softmax(jnp.where(mask, s, -jnp.inf), axis=-1)
    return jnp.einsum('bhqk,bkhd->bqhd', p[0] - lam * p[1], v)


def diff_attention(q, k, v, q_pos, past_k, past_v, lam):
    B, L = q.shape[:2]
    k_rows = k.reshape(B, L, A_HEADS, A_QK)
    if past_k is None:
        k_all, v_all, k_pos = k_rows, v, q_pos
    else:
        k_all = jnp.concatenate([past_k.astype(k_rows.dtype), k_rows], axis=1)
        v_all = jnp.concatenate([past_v.astype(v.dtype), v], axis=1)
        k_pos = jnp.arange(k_all.shape[1], dtype=jnp.int32)
    kf = k_all.astype(jnp.float32).reshape(B, k_all.shape[1], A_HEADS, 2, A_HD)
    vf = v_all.astype(jnp.float32)
    qf = q.astype(jnp.float32) * (A_HD ** -0.5)
    if L > Q_BLOCK and L % Q_BLOCK == 0:
        nb = L // Q_BLOCK
        qb = jnp.moveaxis(qf.reshape(B, nb, Q_BLOCK, A_HEADS, 2, A_HD), 1, 0)
        pb = q_pos.reshape(nb, Q_BLOCK)
        o = lax.map(lambda t: diff_attn_block(t[0], kf, vf, t[1], k_pos, lam), (qb, pb))
        o = jnp.moveaxis(o, 0, 1).reshape(B, L, A_HEADS, A_DV)
    else:
        o = diff_attn_block(qf, kf, vf, q_pos, k_pos, lam)
    return o, k_rows


def gated_delta_chunked(q, k, v, g, beta, S0):
    B, H, L, dk = q.shape
    dv = v.shape[-1]
    C = DN_CHUNK
    pad = (-L) % C
    if pad:
        pw = ((0, 0), (0, 0), (0, pad))
        q = jnp.pad(q, pw + ((0, 0),))
        k = jnp.pad(k, pw + ((0, 0),))
        v = jnp.pad(v, pw + ((0, 0),))
        g = jnp.pad(g, pw)
        beta = jnp.pad(beta, pw)
    n = (L + pad) // C
    q = (q * (dk ** -0.5)).reshape(B, H, n, C, dk)
    k = k.reshape(B, H, n, C, dk)
    v = v.reshape(B, H, n, C, dv)
    gc = jnp.cumsum(g.reshape(B, H, n, C), axis=-1)
    beta = beta.reshape(B, H, n, C)
    idx = jnp.arange(C)
    incl = idx[:, None] >= idx[None, :]
    strict = idx[:, None] > idx[None, :]
    decay = jnp.exp(jnp.where(incl, gc[..., :, None] - gc[..., None, :], -jnp.inf))
    kb = k * beta[..., None]
    lower = jnp.where(strict, jnp.einsum('bhncd,bhnmd->bhncm', kb, k) * decay, 0.0)
    rhs = jnp.concatenate([v * beta[..., None], kb * jnp.exp(gc)[..., None]], axis=-1)
    sol = lax.linalg.triangular_solve(jnp.eye(C, dtype=q.dtype) + lower, rhs, left_side=True, lower=True)
    u, w = sol[..., :dv], sol[..., dv:]
    qk = jnp.einsum('bhncd,bhnmd->bhncm', q, k) * decay
    q_dec = q * jnp.exp(gc)[..., None]
    k_dec = k * jnp.exp(gc[..., -1:] - gc)[..., None]
    g_tot = jnp.exp(gc[..., -1])
    xs = tuple(jnp.moveaxis(t, 2, 0) for t in (u, w, qk, q_dec, k_dec, g_tot))

    def step(S, inp):
        u_i, w_i, qk_i, qd_i, kd_i, gt_i = inp
        v_new = u_i - jnp.einsum('bhcd,bhde->bhce', w_i, S)
        o_i = jnp.einsum('bhcd,bhde->bhce', qd_i, S) + jnp.einsum('bhcm,bhme->bhce', qk_i, v_new)
        S = S * gt_i[..., None, None] + jnp.einsum('bhcd,bhce->bhde', kd_i, v_new)
        return S, o_i

    S, o = lax.scan(step, S0, xs)
    o = jnp.moveaxis(o, 0, 2).reshape(B, H, n * C, dv)[:, :, :L]
    return o, S


def gated_deltanet(qkv, z, a, b, conv_state, S0, conv_w, a_log, dt_bias, norm_w):
    B, L = qkv.shape[:2]
    xp = jnp.concatenate([conv_state.astype(qkv.dtype), qkv], axis=1)
    new_conv = xp[:, L:]
    c = xp[:, 0:L] * conv_w[0]
    for i in range(1, CONV_W):
        c = c + xp[:, i:i + L] * conv_w[i]
    c = jax.nn.silu(c)
    q, k, v = jnp.split(c, [B_K_WIDTH, 2 * B_K_WIDTH], axis=-1)
    q = l2norm(q.reshape(B, L, B_HEADS, B_DK))
    k = l2norm(k.reshape(B, L, B_HEADS, B_DK))
    v = v.reshape(B, L, B_HEADS, B_DV).astype(jnp.float32)
    beta = jax.nn.sigmoid(b.astype(jnp.float32))
    g = -jnp.exp(a_log.astype(jnp.float32)) * jax.nn.softplus(a.astype(jnp.float32) + dt_bias.astype(jnp.float32))
    tr = lambda t: jnp.swapaxes(t, 1, 2)
    o, S = gated_delta_chunked(tr(q), tr(k), tr(v), tr(g), tr(beta), S0.astype(jnp.float32))
    o = rmsnorm(tr(o), norm_w) * jax.nn.silu(z.reshape(B, L, B_HEADS, B_DV).astype(jnp.float32))
    return o.reshape(B, L, B_V_WIDTH).astype(qkv.dtype), new_conv, S.astype(S0.dtype)


def decoder_layer(x, pos, past_k, past_v, conv_state, delta_state, p, layer_idx):
    B, L = x.shape[:2]
    h = x + 0.5 * rmsnorm(swiglu(rmsnorm(x, p['ffn1_pre']), p['ffn1_w_in'], p['ffn1_w_out']), p['ffn1_post'])
    u = rmsnorm(h, p['mix_pre'])
    pts = np.cumsum(IN_SPLITS)[:-1].tolist()
    qa, ka, va, qkv_b, z_b, a_b, b_b, gate_a, gate_b = jnp.split(u @ p['w_in'], pts, axis=-1)
    qa = rope(qa.reshape(B, L, A_HEADS, 2, A_HD), pos)
    ka = rope(ka.reshape(B, L, A_HEADS, 2, A_HD), pos)
    va = va.reshape(B, L, A_HEADS, A_DV)
    lam_init = 0.8 - 0.6 * math.exp(-0.3 * layer_idx)
    f32 = jnp.float32
    lam = (jnp.exp(jnp.sum(p['lq1'].astype(f32) * p['lk1'].astype(f32)))
           - jnp.exp(jnp.sum(p['lq2'].astype(f32) * p['lk2'].astype(f32))) + lam_init)
    o_a, k_rows = diff_attention(qa, ka, va, pos, past_k, past_v, lam)
    o_a = (rmsnorm(o_a, p['attn_subln']) * (1.0 - lam_init)).reshape(B, L, A_V_WIDTH).astype(x.dtype)
    branch_a = o_a @ p['w_a_out']
    o_b, new_conv, new_S = gated_deltanet(qkv_b, z_b, a_b, b_b, conv_state, delta_state, p['conv_w'],
                                          p['a_log'], p['dt_bias'], p['delta_norm'])
    branch_b = o_b @ p['w_b_out']
    merged = jax.nn.sigmoid(gate_a) * branch_a + jax.nn.sigmoid(gate_b) * branch_b
    h = h + rmsnorm(merged @ p['w_o'], p['mix_post'])
    y = h + 0.5 * rmsnorm(swiglu(rmsnorm(h, p['ffn2_pre']), p['ffn2_w_in'], p['ffn2_w_out']), p['ffn2_post'])
    return y, k_rows, va, new_conv, new_S


def setup_inputs(seed: int = 0) -> dict:
    key = jax.random.key(seed)
    ks = iter(jax.random.split(key, 40))
    f32 = jnp.float32

    def normal(shape, scale):
        return scale * jax.random.normal(next(ks), shape, f32)

    def gain(shape):
        return 1.0 + normal(shape, 0.05)

    n_pages = PAST_LEN // PAGE_SIZE
    used = DEC_BATCH * n_pages
    n_pool = used + max(1, used // 4)
    page_table = jax.random.permutation(next(ks), n_pool)[:used].reshape(DEC_BATCH, n_pages).astype(jnp.int32)
    dt = jax.random.uniform(next(ks), (DEPTH, B_HEADS), f32, 1e-3, 0.1)
    return {
        'x_prompt': normal((BATCH, SEQ, D_MODEL), 1.0),
        'x_sample': normal((DEC_BATCH, DEC_SEQ, D_MODEL), 1.0),
        'cache_k': normal((DEPTH, n_pool, PAGE_SIZE, A_HEADS, A_QK), 1.0),
        'cache_v': normal((DEPTH, n_pool, PAGE_SIZE, A_HEADS, A_DV), 1.0),
        'state_conv': normal((DEPTH, DEC_BATCH, CONV_W - 1, CONV_CH), 1.0),
        'state_delta': normal((DEPTH, DEC_BATCH, B_HEADS, B_DK, B_DV), 0.1),
        'page_table': page_table,
        'ffn1_pre_norm': gain((DEPTH, D_MODEL)),
        'ffn1_w_in': normal((DEPTH, D_MODEL, 2 * D_FF), D_MODEL ** -0.5),
        'ffn1_w_out': normal((DEPTH, D_FF, D_MODEL), D_FF ** -0.5),
        'ffn1_post_norm': gain((DEPTH, D_MODEL)),
        'mix_pre_norm': gain((DEPTH, D_MODEL)),
        'w_in': normal((DEPTH, D_MODEL, D_IN), D_MODEL ** -0.5),
        'conv_w': normal((DEPTH, CONV_W, CONV_CH), CONV_W ** -0.5),
        'lambda_q1': normal((DEPTH, A_HD), 0.1),
        'lambda_k1': normal((DEPTH, A_HD), 0.1),
        'lambda_q2': normal((DEPTH, A_HD), 0.1),
        'lambda_k2': normal((DEPTH, A_HD), 0.1),
        'attn_subln': gain((DEPTH, A_DV)),
        'a_log': jnp.log(jax.random.uniform(next(ks), (DEPTH, B_HEADS), f32, 1.0, 16.0)),
        'dt_bias': jnp.log(jnp.expm1(dt)),
        'delta_norm': gain((DEPTH, B_DV)),
        'w_a_out': normal((DEPTH, A_V_WIDTH, D_MODEL), A_V_WIDTH ** -0.5),
        'w_b_out': normal((DEPTH, B_V_WIDTH, D_MODEL), B_V_WIDTH ** -0.5),
        'w_o': normal((DEPTH, D_MODEL, D_MODEL), D_MODEL ** -0.5),
        'mix_post_norm': gain((DEPTH, D_MODEL)),
        'ffn2_pre_norm': gain((DEPTH, D_MODEL)),
        'ffn2_w_in': normal((DEPTH, D_MODEL, 2 * D_FF), D_MODEL ** -0.5),
        'ffn2_w_out': normal((DEPTH, D_FF, D_MODEL), D_FF ** -0.5),
        'ffn2_post_norm': gain((DEPTH, D_MODEL)),
    }


def reference(x_prompt, x_sample, cache_k, cache_v, state_conv, state_delta, page_table,
              ffn1_pre_norm, ffn1_w_in, ffn1_w_out, ffn1_post_norm, mix_pre_norm, w_in, conv_w,
              lambda_q1, lambda_k1, lambda_q2, lambda_k2, attn_subln, a_log, dt_bias, delta_norm,
              w_a_out, w_b_out, w_o, mix_post_norm, ffn2_pre_norm, ffn2_w_in, ffn2_w_out, ffn2_post_norm):
    n_pages = PAST_LEN // PAGE_SIZE
    Bp, Lp = x_prompt.shape[:2]
    Bs, Ls = x_sample.shape[:2]
    pos_p = jnp.arange(Lp, dtype=jnp.int32)
    pos_s = PAST_LEN + jnp.arange(Ls, dtype=jnp.int32)
    yp, ys = x_prompt, x_sample
    kp_l, vp_l, cp_l, sp_l, ks_l, vs_l, cs_l, ss_l = [], [], [], [], [], [], [], []
    for i in range(DEPTH):
        p = {
            'ffn1_pre': ffn1_pre_norm[i], 'ffn1_w_in': ffn1_w_in[i], 'ffn1_w_out': ffn1_w_out[i],
            'ffn1_post': ffn1_post_norm[i], 'mix_pre': mix_pre_norm[i], 'w_in': w_in[i], 'conv_w': conv_w[i],
            'lq1': lambda_q1[i], 'lk1': lambda_k1[i], 'lq2': lambda_q2[i], 'lk2': lambda_k2[i],
            'attn_subln': attn_subln[i], 'a_log': a_log[i], 'dt_bias': dt_bias[i], 'delta_norm': delta_norm[i],
            'w_a_out': w_a_out[i], 'w_b_out': w_b_out[i], 'w_o': w_o[i], 'mix_post': mix_post_norm[i],
            'ffn2_pre': ffn2_pre_norm[i], 'ffn2_w_in': ffn2_w_in[i], 'ffn2_w_out': ffn2_w_out[i],
            'ffn2_post': ffn2_post_norm[i],
        }
        zero_conv = jnp.zeros((Bp, CONV_W - 1, CONV_CH), x_prompt.dtype)
        zero_S = jnp.zeros((Bp, B_HEADS, B_DK, B_DV), state_delta.dtype)
        yp, kp, vp, cp, sp = decoder_layer(yp, pos_p, None, None, zero_conv, zero_S, p, i)
        past_k = cache_k[i][page_table].reshape(Bs, n_pages * PAGE_SIZE, A_HEADS, A_QK)
        past_v = cache_v[i][page_table].reshape(Bs, n_pages * PAGE_SIZE, A_HEADS, A_DV)
        ys, ks_, vs_, cs_, ss_ = decoder_layer(ys, pos_s, past_k, past_v, state_conv[i], state_delta[i], p, i)
        kp_l.append(kp); vp_l.append(vp); cp_l.append(cp); sp_l.append(sp)
        ks_l.append(ks_); vs_l.append(vs_); cs_l.append(cs_); ss_l.append(ss_)
    return (yp, ys, jnp.stack(kp_l), jnp.stack(vp_l), jnp.stack(cp_l), jnp.stack(sp_l),
            jnp.stack(ks_l), jnp.stack(vs_l), jnp.stack(cs_l), jnp.stack(ss_l))
```

```python
import functools
import math

import numpy as np
import jax
import jax.numpy as jnp
from jax import lax
from jax.experimental import pallas as pl
from jax.experimental.pallas import tpu as pltpu

F32 = jnp.float32
BF16 = jnp.bfloat16

NORM_EPS = 1e-6
ROPE_THETA = 10000.0
A_HEADS = 8
A_HD = 64
HEAD_W = 2 * A_HD
B_HEADS = 8
B_DK = 128
CONV_W = 4
DN_CHUNK = 64
NEG_BIG = -1e30
LOG2E = math.log2(math.e)
VMEM_LIMIT_BYTES = 56 * 1024 * 1024
HIGHEST = lax.Precision.HIGHEST


def _cparams(sem):
    return pltpu.CompilerParams(dimension_semantics=sem, vmem_limit_bytes=VMEM_LIMIT_BYTES)


def _const_spec(shape):
    zeros = (0,) * len(shape)
    return pl.BlockSpec(shape, lambda *_: zeros, pipeline_mode=pl.Buffered(1))


def _rms(x, w):
    return x * lax.rsqrt(jnp.mean(x * x, axis=-1, keepdims=True) + NORM_EPS) * w


def _dot(a, b, precision=None):
    return jnp.dot(a, b, preferred_element_type=F32, precision=precision)


def _dot_nt(a, b, precision=None):
    return lax.dot_general(a, b, (((1,), (1,)), ((), ())), preferred_element_type=F32, precision=precision)


def _dot_tn(a, b, precision=None):
    return lax.dot_general(a, b, (((0,), (0,)), ((), ())), preferred_element_type=F32, precision=precision)


def _ffn_kernel(x_ref, pre_ref, wg_ref, wu_ref, wo_ref, post_ref, o_ref, *, n_chunk):
    x = x_ref[...]
    xn = _rms(x, pre_ref[...]).astype(BF16)
    ck = wg_ref.shape[1] // n_chunk
    acc = None
    for c in range(n_chunk):
        g = _dot(xn, wg_ref[:, c * ck:(c + 1) * ck])
        u = _dot(xn, wu_ref[:, c * ck:(c + 1) * ck])
        a = (g * jax.nn.sigmoid(g) * u).astype(BF16)
        part = _dot(a, wo_ref[c * ck:(c + 1) * ck, :])
        acc = part if acc is None else acc + part
    o_ref[...] = x + 0.5 * _rms(acc, post_ref[...])


def _ffn(x, pre, wg, wu, wo, post, *, tm):
    t, d = x.shape
    f = wg.shape[1]
    n_chunk = 2 if f % 256 == 0 else 1
    return pl.pallas_call(
        functools.partial(_ffn_kernel, n_chunk=n_chunk),
        grid=(t // tm,),
        in_specs=[
            pl.BlockSpec((tm, d), lambda i: (i, 0)),
            _const_spec((1, d)), _const_spec((d, f)), _const_spec((d, f)), _const_spec((f, d)),
            _const_spec((1, d)),
        ],
        out_specs=pl.BlockSpec((tm, d), lambda i: (i, 0)),
        out_shape=jax.ShapeDtypeStruct((t, d), F32),
        compiler_params=_cparams(("parallel",)),
        name="ffn",
    )(x, pre, wg, wu, wo, post)


def _rope_heads(x, cos, sin_lo, sin_hi):
    outs = []
    for h in range(x.shape[1] // HEAD_W):
        xh = x[:, h * HEAD_W:(h + 1) * HEAD_W]
        up = pltpu.roll(xh, HEAD_W - A_HD // 2, axis=1)
        dn = pltpu.roll(xh, A_HD // 2, axis=1)
        outs.append(xh * cos + up * sin_lo + dn * sin_hi)
    return outs


def _proj_kernel(h_ref, pre_ref, wq_ref, wk_ref, wv_ref, wc_ref, wz_ref, wab_ref, wga_ref, wgb_ref,
                 cos_ref, slo_ref, shi_ref, alog_ref, dt_ref,
                 q_ref, kf_ref, kb_ref, vf_ref, vb_ref, c_ref, z_ref, gb_ref, ga_ref, gbt_ref):
    u = _rms(h_ref[...], pre_ref[...]).astype(BF16)
    cos, slo, shi = cos_ref[...], slo_ref[...], shi_ref[...]
    q_scale = (A_HD ** -0.5) * LOG2E
    for h, qh in enumerate(_rope_heads(_dot(u, wq_ref[...]), cos, slo, shi)):
        q_ref[:, h * HEAD_W:(h + 1) * HEAD_W] = (qh * q_scale).astype(BF16)
    for h, kh in enumerate(_rope_heads(_dot(u, wk_ref[...]), cos, slo, shi)):
        kf_ref[:, h * HEAD_W:(h + 1) * HEAD_W] = kh
        kb_ref[:, h * HEAD_W:(h + 1) * HEAD_W] = kh.astype(BF16)
    v = _dot(u, wv_ref[...])
    vf_ref[...] = v
    vb_ref[...] = v.astype(BF16)
    n_c = wc_ref.shape[1] // wq_ref.shape[1]
    for j in range(n_c):
        w = wq_ref.shape[1]
        c_ref[:, j * w:(j + 1) * w] = _dot(u, wc_ref[:, j * w:(j + 1) * w])
    z_ref[...] = _dot(u, wz_ref[...]).astype(BF16)
    ga_ref[...] = _dot(u, wga_ref[...]).astype(BF16)
    gbt_ref[...] = _dot(u, wgb_ref[...]).astype(BF16)
    ab = _dot(u, wab_ref[...])
    xs = ab + dt_ref[...]
    softplus = jnp.maximum(xs, 0.0) + jnp.log(1.0 + jnp.exp(-jnp.abs(xs)))
    gval = -jnp.exp(alog_ref[...]) * softplus
    lane = lax.broadcasted_iota(jnp.int32, ab.shape, 1)
    gb_ref[...] = jnp.where(lane < B_HEADS, gval, jnp.where(lane < 2 * B_HEADS, jax.nn.sigmoid(ab), 0.0))


def _proj(h, pre, ws, tabs, alog_row, dt_row, *, tm):
    t, d = h.shape
    wq, wk, wv, wc, wz, wab, wga, wgb = ws
    cos, slo, shi = tabs
    nt = cos.shape[0] // tm
    row = lambda w: pl.BlockSpec((tm, w), lambda i: (i, 0))
    tab = pl.BlockSpec((tm, HEAD_W), lambda i: (i % nt, 0))
    sd = lambda w, dt: jax.ShapeDtypeStruct((t, w), dt)
    cw = wc.shape[1]
    return pl.pallas_call(
        _proj_kernel,
        grid=(t // tm,),
        in_specs=[row(d), _const_spec((1, d))] + [_const_spec(w.shape) for w in ws]
                 + [tab, tab, tab, _const_spec((1, HEAD_W)), _const_spec((1, HEAD_W))],
        out_specs=[row(d), row(d), row(d), row(d), row(d), row(cw), row(d), row(HEAD_W), row(d), row(d)],
        out_shape=[sd(d, BF16), sd(d, F32), sd(d, BF16), sd(d, F32), sd(d, BF16), sd(cw, F32), sd(d, BF16),
                   sd(HEAD_W, F32), sd(d, BF16), sd(d, BF16)],
        compiler_params=_cparams(("parallel",)),
        name="proj",
    )(h, pre, *ws, cos, slo, shi, alog_row, dt_row)


def _lambda_full(lam_ref, lam_init):
    l = lam_ref[...]
    s1 = jnp.sum(l[0:1] * l[1:2], axis=-1, keepdims=True)
    s2 = jnp.sum(l[2:3] * l[3:4], axis=-1, keepdims=True)
    return jnp.exp(s1) - jnp.exp(s2) + lam_init


def _attn_kernel(qi_ref, ki_ref, lam_ref, sub_ref, q_ref, k_ref, v_ref, o_ref,
                 q1_s, q2_s, m_s, l_s, acc_s, *, lam_init):
    p = pl.program_id(2)
    qi = qi_ref[p]
    ki = ki_ref[p]
    tq, tk = q_ref.shape[1], k_ref.shape[1]

    @pl.when(ki == 0)
    def _init():
        q = q_ref[0]
        lane = lax.broadcasted_iota(jnp.int32, q.shape, 1)
        zero = jnp.zeros_like(q)
        q1_s[...] = jnp.where(lane < A_HD, q, zero)
        q2_s[...] = jnp.where(lane >= A_HD, q, zero)
        m_s[...] = jnp.full(m_s.shape, NEG_BIG, F32)
        l_s[...] = jnp.zeros(l_s.shape, F32)
        acc_s[...] = jnp.zeros(acc_s.shape, F32)

    def _step(masked):
        k = k_ref[0]
        v = v_ref[0]
        if masked:
            row = lax.broadcasted_iota(jnp.int32, (tq, tk), 0)
            col = lax.broadcasted_iota(jnp.int32, (tq, tk), 1)
            keep = col <= row
        for m, qs in enumerate((q1_s, q2_s)):
            s = _dot_nt(qs[...], k)
            if masked:
                s = jnp.where(keep, s, NEG_BIG)
            m_prev = m_s[m]
            m_new = jnp.maximum(m_prev, jnp.max(s, axis=-1, keepdims=True))
            alpha = jnp.exp2(m_prev - m_new)
            pm = jnp.exp2(s - m_new)
            l_s[m] = alpha * l_s[m] + jnp.sum(pm, axis=-1, keepdims=True)
            acc_s[m] = alpha * acc_s[m] + _dot(pm.astype(BF16), v)
            m_s[m] = m_new

    @pl.when(ki < qi)
    def _off():
        _step(False)

    @pl.when(ki == qi)
    def _diag():
        _step(True)
        lam = _lambda_full(lam_ref, lam_init)
        o = acc_s[0] / l_s[0] - lam * (acc_s[1] / l_s[1])
        o_ref[0] = (_rms(o, sub_ref[...]) * (1.0 - lam_init)).astype(o_ref.dtype)


def _attn_prompt(q, k, v, lamv, subln, *, lam_init, tq):
    b, l, _ = q.shape
    nq = l // tq
    pairs = [(i, j) for i in range(nq) for j in range(i + 1)]
    qi_tab = jnp.asarray(np.array([pq for pq, _ in pairs], np.int32))
    ki_tab = jnp.asarray(np.array([pk for _, pk in pairs], np.int32))
    q_spec = pl.BlockSpec((1, tq, HEAD_W), lambda bi, h, p, qt, kt: (bi, qt[p], h))
    kv_spec = pl.BlockSpec((1, tq, HEAD_W), lambda bi, h, p, qt, kt: (bi, kt[p], h))
    grid_spec = pltpu.PrefetchScalarGridSpec(
        num_scalar_prefetch=2,
        grid=(b, A_HEADS, len(pairs)),
        in_specs=[
            pl.BlockSpec((4, A_HD), lambda bi, h, p, qt, kt: (0, 0)),
            pl.BlockSpec((1, HEAD_W), lambda bi, h, p, qt, kt: (0, 0)),
            q_spec, kv_spec, kv_spec,
        ],
        out_specs=q_spec,
        scratch_shapes=[
            pltpu.VMEM((tq, HEAD_W), BF16), pltpu.VMEM((tq, HEAD_W), BF16),
            pltpu.VMEM((2, tq, 1), F32), pltpu.VMEM((2, tq, 1), F32), pltpu.VMEM((2, tq, HEAD_W), F32),
        ],
    )
    return pl.pallas_call(
        functools.partial(_attn_kernel, lam_init=lam_init),
        grid_spec=grid_spec,
        out_shape=jax.ShapeDtypeStruct(q.shape, BF16),
        compiler_params=_cparams(("parallel", "parallel", "arbitrary")),
        name="attn_prompt",
    )(qi_tab, ki_tab, lamv, subln, q, k, v)


N_QROWS = 8
NEW_TOK_PAD = 16


def _attn_decode_kernel(pt_ref, lam_ref, sub_ref, q_ref, kn_ref, vn_ref, kc_ref, vc_ref, o_ref,
                        m_s, l_s, acc_s, *, lam_init, n_new):
    pg = pl.program_id(1)
    n_pages = pl.num_programs(1)
    qall = q_ref[0]
    nr = qall.shape[0]

    def _update(k2d, v2d, causal):
        s = _dot_nt(qall, k2d)
        row = lax.broadcasted_iota(jnp.int32, s.shape, 0)
        col = lax.broadcasted_iota(jnp.int32, s.shape, 1)
        keep = (col % A_HEADS) == (row // N_QROWS)
        if causal:
            keep = jnp.logical_and(keep, (col // A_HEADS) <= (row % n_new))
        s = jnp.where(keep, s, NEG_BIG)
        m_prev = m_s[...]
        m_new = jnp.maximum(m_prev, jnp.max(s, axis=-1, keepdims=True))
        alpha = jnp.exp2(m_prev - m_new)
        pm = jnp.exp2(s - m_new)
        l_s[...] = alpha * l_s[...] + jnp.sum(pm, axis=-1, keepdims=True)
        acc_s[...] = alpha * acc_s[...] + _dot(pm.astype(BF16), v2d)
        m_s[...] = m_new

    @pl.when(pg == 0)
    def _new_rows():
        m_s[...] = jnp.full(m_s.shape, NEG_BIG, F32)
        l_s[...] = jnp.zeros(l_s.shape, F32)
        acc_s[...] = jnp.zeros(acc_s.shape, F32)
        _update(kn_ref[0], vn_ref[0], True)

    rows = kc_ref.shape[1] * kc_ref.shape[2]
    _update(kc_ref[0].reshape(rows, HEAD_W).astype(BF16), vc_ref[0].reshape(rows, HEAD_W).astype(BF16), False)

    @pl.when(pg == n_pages - 1)
    def _fin():
        lam = _lambda_full(lam_ref, lam_init)
        nrm = acc_s[...] / l_s[...]
        o = nrm - lam * pltpu.roll(nrm, nr - N_QROWS // 2, axis=0)
        o_ref[0] = (_rms(o, sub_ref[...]) * (1.0 - lam_init)).astype(o_ref.dtype)


def _attn_decode(qall, kn, vn, cache_k, cache_v, page_table, lamv, subln, *, lam_init, n_new):
    bs, n_pages = page_table.shape
    page = cache_k.shape[1]
    nr = qall.shape[1]
    seq = lambda r: pl.BlockSpec((1, r, HEAD_W), lambda b, p, pt: (b, 0, 0))
    cache = pl.BlockSpec((1, page, A_HEADS, HEAD_W), lambda b, p, pt: (pt[b, p], 0, 0, 0))
    grid_spec = pltpu.PrefetchScalarGridSpec(
        num_scalar_prefetch=1,
        grid=(bs, n_pages),
        in_specs=[
            pl.BlockSpec((4, A_HD), lambda b, p, pt: (0, 0)),
            pl.BlockSpec((1, HEAD_W), lambda b, p, pt: (0, 0)),
            seq(nr), seq(kn.shape[1]), seq(vn.shape[1]), cache, cache,
        ],
        out_specs=seq(nr),
        scratch_shapes=[pltpu.VMEM((nr, 1), F32), pltpu.VMEM((nr, 1), F32), pltpu.VMEM((nr, HEAD_W), F32)],
    )
    return pl.pallas_call(
        functools.partial(_attn_decode_kernel, lam_init=lam_init, n_new=n_new),
        grid_spec=grid_spec,
        out_shape=jax.ShapeDtypeStruct((bs, nr, HEAD_W), BF16),
        compiler_params=_cparams(("parallel", "arbitrary")),
        name="attn_decode",
    )(page_table, lamv, subln, qall, kn, vn, cache_k, cache_v)


CONV_PAD = 8


def _delta_kernel(x_ref, conv0_ref, gb_ref, z_ref, s0_ref, cw_ref, nw_ref, o_ref, sout_ref, xp_s, st_s):
    j = pl.program_id(1)
    c = x_ref.shape[1]
    nh = st_s.shape[0]
    dk = st_s.shape[1]
    kw = nh * dk

    @pl.when(j == 0)
    def _init():
        xp_s[0:CONV_PAD, :] = conv0_ref[0]
        st_s[...] = s0_ref[0]

    xp_s[CONV_PAD:CONV_PAD + c, :] = x_ref[0]

    def conv(col):
        acc = None
        for i in range(CONV_W):
            r0 = CONV_PAD - (CONV_W - 1) + i
            term = xp_s[r0:r0 + c, col:col + dk] * cw_ref[i:i + 1, col:col + dk]
            acc = term if acc is None else acc + term
        return acc * jax.nn.sigmoid(acc)

    def l2n(x):
        return x * lax.rsqrt(jnp.sum(x * x, axis=-1, keepdims=True) + NORM_EPS)

    gb = gb_ref[0]
    ri = lax.broadcasted_iota(jnp.int32, (c, c), 0)
    ci = lax.broadcasted_iota(jnp.int32, (c, c), 1)
    incl = ri >= ci
    strict = ri > ci
    ltri = jnp.where(incl, 1.0, 0.0).astype(F32)
    eye = jnp.where(ri == ci, 1.0, 0.0).astype(F32)
    gc_all = _dot(ltri, gb, HIGHEST)
    n_double = int(math.log2(c)) - 1

    for h in range(nh):
        q = l2n(conv(h * dk)) * (dk ** -0.5)
        k = l2n(conv(kw + h * dk))
        v = conv(2 * kw + h * dk)
        g_col = gb[:, h:h + 1]
        beta = gb[:, nh + h:nh + h + 1]
        gc = gc_all[:, h:h + 1]
        gc_last = gc_all[c - 1:c, h:h + 1]
        gd = _dot(ltri, jnp.where(strict, g_col, 0.0), HIGHEST)
        decay = jnp.where(incl, jnp.exp(gd), 0.0)
        kbeta = k * beta
        a = jnp.where(strict, _dot_nt(kbeta, k, HIGHEST) * decay, 0.0)
        tinv = eye - a
        apow = a
        for _ in range(n_double):
            apow = _dot(apow, apow, HIGHEST)
            tinv = tinv + _dot(tinv, apow, HIGHEST)
        e_gc = jnp.exp(gc)
        u = _dot(tinv, v * beta, HIGHEST)
        w = _dot(tinv, kbeta * e_gc, HIGHEST)
        qk = _dot_nt(q, k, HIGHEST) * decay
        q_dec = q * e_gc
        k_dec = k * jnp.exp(gc_last - gc)
        st = st_s[h]
        st_b = st.astype(BF16)
        v_new = u - _dot(w.astype(BF16), st_b)
        v_new_b = v_new.astype(BF16)
        o = _dot(q_dec.astype(BF16), st_b) + _dot(qk.astype(BF16), v_new_b)
        st_s[h] = st * jnp.exp(gc_last) + _dot_tn(k_dec.astype(BF16), v_new_b)
        zh = z_ref[0, :, h * dk:(h + 1) * dk].astype(F32)
        o_ref[0, :, h * dk:(h + 1) * dk] = (_rms(o, nw_ref[...]) * (zh * jax.nn.sigmoid(zh))).astype(o_ref.dtype)

    xp_s[0:CONV_PAD, :] = xp_s[c:c + CONV_PAD, :]

    @pl.when(j == pl.num_programs(1) - 1)
    def _fin():
        sout_ref[0] = st_s[...]


def _delta(x, conv0, gb, z, s0, conv_w, norm_w, *, chunk):
    b, l, cc = x.shape
    nh, dk, dv = s0.shape[1:]
    blk = lambda w: pl.BlockSpec((1, chunk, w), lambda bi, j: (bi, j, 0))
    return pl.pallas_call(
        _delta_kernel,
        grid=(b, l // chunk),
        in_specs=[
            blk(cc),
            pl.BlockSpec((1, CONV_PAD, cc), lambda bi, j: (bi, 0, 0)),
            blk(gb.shape[2]), blk(z.shape[2]),
            pl.BlockSpec((1, nh, dk, dv), lambda bi, j: (bi, 0, 0, 0)),
            pl.BlockSpec((CONV_W, cc), lambda bi, j: (0, 0)),
            pl.BlockSpec((1, dv), lambda bi, j: (0, 0)),
        ],
        out_specs=[blk(nh * dv), pl.BlockSpec((1, nh, dk, dv), lambda bi, j: (bi, 0, 0, 0))],
        out_shape=[jax.ShapeDtypeStruct((b, l, nh * dv), BF16), jax.ShapeDtypeStruct(s0.shape, F32)],
        scratch_shapes=[pltpu.VMEM((chunk + CONV_PAD, cc), F32), pltpu.VMEM((nh, dk, dv), F32)],
        compiler_params=_cparams(("parallel", "arbitrary")),
        name="delta",
    )(x, conv0, gb, z, s0, conv_w, norm_w)


def _merge_kernel(h_ref, oa_ref, ob_ref, ga_ref, gb_ref, wa_ref, wb_ref, wo_ref, post_ref, o_ref):
    a = _dot(oa_ref[...], wa_ref[...])
    b = _dot(ob_ref[...], wb_ref[...])
    m = jax.nn.sigmoid(ga_ref[...].astype(F32)) * a + jax.nn.sigmoid(gb_ref[...].astype(F32)) * b
    r = _dot(m.astype(BF16), wo_ref[...])
    o_ref[...] = h_ref[...] + _rms(r, post_ref[...])


def _merge(h, oa, ob, ga, gb, wa, wb, wo, post, *, tm):
    t, d = h.shape
    row = pl.BlockSpec((tm, d), lambda i: (i, 0))
    return pl.pallas_call(
        _merge_kernel,
        grid=(t // tm,),
        in_specs=[row, row, row, row, row, _const_spec(wa.shape), _const_spec(wb.shape), _const_spec(wo.shape),
                  _const_spec((1, d))],
        out_specs=row,
        out_shape=jax.ShapeDtypeStruct((t, d), F32),
        compiler_params=_cparams(("parallel",)),
        name="merge",
    )(h, oa, ob, ga, gb, wa, wb, wo, post)


def _rope_tables(pos):
    half = A_HD // 2
    inv = jnp.power(ROPE_THETA, -2.0 * jnp.arange(half, dtype=F32) / A_HD)
    ang = pos.astype(F32)[:, None] * inv[None, :]
    cos, sin = jnp.cos(ang), jnp.sin(ang)
    zero = jnp.zeros_like(sin)
    reps = HEAD_W // A_HD
    cos_t = jnp.tile(jnp.concatenate([cos, cos], axis=1), (1, reps))
    sin_lo = jnp.tile(jnp.concatenate([-sin, zero], axis=1), (1, reps))
    sin_hi = jnp.tile(jnp.concatenate([zero, sin], axis=1), (1, reps))
    return cos_t, sin_lo, sin_hi


def _pad_rows(x, rows, front=False):
    pad = rows - x.shape[1]
    cfg = [(0, 0)] * x.ndim
    cfg[1] = (pad, 0) if front else (0, pad)
    return jnp.pad(x, cfg)


def _layer(x, tabs, p, lam_init, *, tm, past):
    b, l, d = x.shape
    t = b * l
    h1 = _ffn(x.reshape(t, d), p['ffn1_pre'], p['ffn1_wg'], p['ffn1_wu'], p['ffn1_wo'], p['ffn1_post'], tm=tm)
    q, kf, kb, vf, vb, c, z, gb, ga, gbt = _proj(h1, p['mix_pre'], p['w_in'], tabs, p['alog'], p['dt'], tm=tm)
    cc = c.shape[1]
    c3 = c.reshape(b, l, cc)
    new_conv = c3[:, l - (CONV_W - 1):]
    if past is None:
        o_a = _attn_prompt(q.reshape(b, l, d), kb.reshape(b, l, d), vb.reshape(b, l, d), p['lamv'], p['subln'],
                           lam_init=lam_init, tq=min(512, l))
        conv0 = jnp.zeros((b, CONV_PAD, cc), F32)
        s0 = jnp.zeros((b, B_HEADS, B_DK, B_DK), F32)
        o_b, new_s = _delta(c3, conv0, gb.reshape(b, l, -1), z.reshape(b, l, d), s0, p['conv_w'], p['dnorm'],
                            chunk=DN_CHUNK)
        o_a = o_a.reshape(t, d)
        o_b = o_b.reshape(t, d)
    else:
        cache_k, cache_v, page_table, conv_state, s_state = past
        assert 2 * l == N_QROWS
        qh = q.reshape(b, l, A_HEADS, HEAD_W).transpose(0, 2, 1, 3)[:, :, None]
        lane_map = (jnp.arange(HEAD_W) // A_HD).reshape(1, 1, 1, 1, HEAD_W)
        qall = jnp.where(lane_map == jnp.arange(2).reshape(1, 1, 2, 1, 1), qh, jnp.zeros((), q.dtype))
        qall = qall.reshape(b, A_HEADS * N_QROWS, HEAD_W)
        new_rows = lambda a: _pad_rows(a.reshape(b, l, A_HEADS, HEAD_W), NEW_TOK_PAD).reshape(b, -1, HEAD_W)
        o_a = _attn_decode(qall, new_rows(kb), new_rows(vb), cache_k, cache_v, page_table, p['lamv'], p['subln'],
                           lam_init=lam_init, n_new=l)
        o_a = o_a.reshape(b, A_HEADS, N_QROWS, HEAD_W)[:, :, :l].transpose(0, 2, 1, 3).reshape(t, d)
        lp = DN_CHUNK
        o_b, new_s = _delta(_pad_rows(c3, lp), _pad_rows(conv_state, CONV_PAD, front=True),
                            _pad_rows(gb.reshape(b, l, -1), lp), _pad_rows(z.reshape(b, l, d), lp),
                            s_state, p['conv_w'], p['dnorm'], chunk=DN_CHUNK)
        o_b = o_b[:, :l].reshape(t, d)
    h2 = _merge(h1, o_a, o_b, ga, gbt, p['w_a'], p['w_b'], p['w_o'], p['mix_post'], tm=tm)
    y = _ffn(h2, p['ffn2_pre'], p['ffn2_wg'], p['ffn2_wu'], p['ffn2_wo'], p['ffn2_post'], tm=tm)
    k_rows = kf.reshape(b, l, A_HEADS, HEAD_W)
    v_rows = vf.reshape(b, l, A_HEADS, HEAD_W)
    return y.reshape(b, l, d), k_rows, v_rows, new_conv, new_s


def kernel(x_prompt, x_sample, cache_k, cache_v, state_conv, state_delta, page_table, ffn1_pre_norm, ffn1_w_in, ffn1_w_out, ffn1_post_norm, mix_pre_norm, w_in, conv_w, lambda_q1, lambda_k1, lambda_q2, lambda_k2, attn_subln, a_log, dt_bias, delta_norm, w_a_out, w_b_out, w_o, mix_post_norm, ffn2_pre_norm, ffn2_w_in, ffn2_w_out, ffn2_post_norm):
    depth = w_in.shape[0]
    d = x_prompt.shape[-1]
    lp, ls = x_prompt.shape[1], x_sample.shape[1]
    n_pages = page_table.shape[1]
    page = cache_k.shape[2]
    past_len = n_pages * page
    tabs_p = _rope_tables(jnp.arange(lp, dtype=jnp.int32))
    tabs_s1 = _rope_tables(past_len + jnp.arange(ls, dtype=jnp.int32))
    tabs_s = tuple(jnp.tile(tb, (x_sample.shape[0], 1)) for tb in tabs_s1)
    kw = B_HEADS * B_DK
    splits = np.cumsum([d, d, d, 3 * kw, kw, B_HEADS, B_HEADS, d, d])[:-1].tolist()
    yp, ys = x_prompt, x_sample
    outs = [[] for _ in range(8)]
    for i in range(depth):
        lam_init = 0.8 - 0.6 * math.exp(-0.3 * i)
        wq, wk, wv, wc, wz, wa_, wb_, wga, wgb = jnp.split(w_in[i], splits, axis=1)
        wab = jnp.pad(jnp.concatenate([wa_, wb_], axis=1), ((0, 0), (0, HEAD_W - 2 * B_HEADS)))
        row = lambda v: v.reshape(1, -1).astype(F32)
        lane_pad = lambda v: jnp.pad(v.reshape(1, -1).astype(F32), ((0, 0), (0, HEAD_W - v.shape[-1])))
        f = ffn1_w_in.shape[2] // 2
        p = {
            'ffn1_pre': row(ffn1_pre_norm[i]), 'ffn1_post': row(ffn1_post_norm[i]),
            'ffn1_wg': ffn1_w_in[i, :, :f].astype(BF16), 'ffn1_wu': ffn1_w_in[i, :, f:].astype(BF16),
            'ffn1_wo': ffn1_w_out[i].astype(BF16),
            'ffn2_pre': row(ffn2_pre_norm[i]), 'ffn2_post': row(ffn2_post_norm[i]),
            'ffn2_wg': ffn2_w_in[i, :, :f].astype(BF16), 'ffn2_wu': ffn2_w_in[i, :, f:].astype(BF16),
            'ffn2_wo': ffn2_w_out[i].astype(BF16),
            'mix_pre': row(mix_pre_norm[i]), 'mix_post': row(mix_post_norm[i]),
            'w_in': tuple(w.astype(BF16) for w in (wq, wk, wv, wc, wz, wab, wga, wgb)),
            'alog': lane_pad(a_log[i]), 'dt': lane_pad(dt_bias[i]),
            'lamv': jnp.stack([lambda_q1[i], lambda_k1[i], lambda_q2[i], lambda_k2[i]]).astype(F32),
            'subln': row(attn_subln[i]), 'dnorm': row(delta_norm[i]),
            'conv_w': conv_w[i].astype(F32),
            'w_a': w_a_out[i].astype(BF16), 'w_b': w_b_out[i].astype(BF16), 'w_o': w_o[i].astype(BF16),
        }
        yp, kp, vp, cp, sp = _layer(yp, tabs_p, p, lam_init, tm=256, past=None)
        ys, ks_, vs_, cs_, ss_ = _layer(ys, tabs_s, p, lam_init, tm=x_sample.shape[0] * ls,
                                        past=(cache_k[i], cache_v[i], page_table, state_conv[i], state_delta[i]))
        for lst, val in zip(outs, (kp, vp, cp, sp, ks_, vs_, cs_, ss_)):
            lst.append(val)
    return (yp, ys) + tuple(jnp.stack(o) for o in outs)
```

```python
import functools
import math

import numpy as np
import jax
import jax.numpy as jnp
from jax import lax
from jax.experimental import pallas as pl
from jax.experimental.pallas import tpu as pltpu

F32 = jnp.float32
BF16 = jnp.bfloat16

NORM_EPS = 1e-6
ROPE_THETA = 10000.0
A_HEADS = 8
A_HD = 64
HEAD_W = 2 * A_HD
B_HEADS = 8
B_DK = 128
CONV_W = 4
DN_CHUNK = 64
NEG_BIG = -1e30
LOG2E = math.log2(math.e)
VMEM_LIMIT_BYTES = 56 * 1024 * 1024
HIGHEST = lax.Precision.HIGHEST


def _cparams(sem):
    return pltpu.CompilerParams(dimension_semantics=sem, vmem_limit_bytes=VMEM_LIMIT_BYTES)


def _const_spec(shape):
    zeros = (0,) * len(shape)
    return pl.BlockSpec(shape, lambda *_: zeros, pipeline_mode=pl.Buffered(1))


def _rms(x, w):
    return x * lax.rsqrt(jnp.mean(x * x, axis=-1, keepdims=True) + NORM_EPS) * w


def _dot(a, b, precision=None):
    return jnp.dot(a, b, preferred_element_type=F32, precision=precision)


def _dot_nt(a, b, precision=None):
    return lax.dot_general(a, b, (((1,), (1,)), ((), ())), preferred_element_type=F32, precision=precision)


def _dot_tn(a, b, precision=None):
    return lax.dot_general(a, b, (((0,), (0,)), ((), ())), preferred_element_type=F32, precision=precision)


def _ffn_kernel(x_ref, pre_ref, wg_ref, wu_ref, wo_ref, post_ref, o_ref, *, n_chunk):
    x = x_ref[...]
    xn = _rms(x, pre_ref[...]).astype(BF16)
    ck = wg_ref.shape[1] // n_chunk
    acc = None
    for c in range(n_chunk):
        g = _dot(xn, wg_ref[:, c * ck:(c + 1) * ck])
        u = _dot(xn, wu_ref[:, c * ck:(c + 1) * ck])
        a = (g * jax.nn.sigmoid(g) * u).astype(BF16)
        part = _dot(a, wo_ref[c * ck:(c + 1) * ck, :])
        acc = part if acc is None else acc + part
    o_ref[...] = x + 0.5 * _rms(acc, post_ref[...])


def _ffn(x, pre, wg, wu, wo, post, *, tm):
    t, d = x.shape
    f = wg.shape[1]
    n_chunk = 2 if f % 256 == 0 else 1
    return pl.pallas_call(
        functools.partial(_ffn_kernel, n_chunk=n_chunk),
        grid=(t // tm,),
        in_specs=[
            pl.BlockSpec((tm, d), lambda i: (i, 0)),
            _const_spec((1, d)), _const_spec((d, f)), _const_spec((d, f)), _const_spec((f, d)),
            _const_spec((1, d)),
        ],
        out_specs=pl.BlockSpec((tm, d), lambda i: (i, 0)),
        out_shape=jax.ShapeDtypeStruct((t, d), F32),
        compiler_params=_cparams(("parallel",)),
        name="ffn",
    )(x, pre, wg, wu, wo, post)


def _rope_heads(x, cos, sin_lo, sin_hi):
    outs = []
    for h in range(x.shape[1] // HEAD_W):
        xh = x[:, h * HEAD_W:(h + 1) * HEAD_W]
        up = pltpu.roll(xh, HEAD_W - A_HD // 2, axis=1)
        dn = pltpu.roll(xh, A_HD // 2, axis=1)
        outs.append(xh * cos + up * sin_lo + dn * sin_hi)
    return outs


def _proj_kernel(h_ref, pre_ref, wq_ref, wk_ref, wv_ref, wc_ref, wz_ref, wab_ref, wga_ref, wgb_ref,
                 cos_ref, slo_ref, shi_ref, alog_ref, dt_ref,
                 q_ref, kf_ref, kb_ref, vf_ref, vb_ref, c_ref, z_ref, gb_ref, ga_ref, gbt_ref):
    u = _rms(h_ref[...], pre_ref[...]).astype(BF16)
    cos, slo, shi = cos_ref[...], slo_ref[...], shi_ref[...]
    q_scale = (A_HD ** -0.5) * LOG2E
    for h, qh in enumerate(_rope_heads(_dot(u, wq_ref[...]), cos, slo, shi)):
        q_ref[:, h * HEAD_W:(h + 1) * HEAD_W] = (qh * q_scale).astype(BF16)
    for h, kh in enumerate(_rope_heads(_dot(u, wk_ref[...]), cos, slo, shi)):
        kf_ref[:, h * HEAD_W:(h + 1) * HEAD_W] = kh
        kb_ref[:, h * HEAD_W:(h + 1) * HEAD_W] = kh.astype(BF16)
    v = _dot(u, wv_ref[...])
    vf_ref[...] = v
    vb_ref[...] = v.astype(BF16)
    n_c = wc_ref.shape[1] // wq_ref.shape[1]
    for j in range(n_c):
        w = wq_ref.shape[1]
        c_ref[:, j * w:(j + 1) * w] = _dot(u, wc_ref[:, j * w:(j + 1) * w])
    z_ref[...] = _dot(u, wz_ref[...]).astype(BF16)
    ga_ref[...] = _dot(u, wga_ref[...]).astype(BF16)
    gbt_ref[...] = _dot(u, wgb_ref[...]).astype(BF16)
    ab = _dot(u, wab_ref[...])
    xs = ab + dt_ref[...]
    softplus = jnp.maximum(xs, 0.0) + jnp.log(1.0 + jnp.exp(-jnp.abs(xs)))
    gval = -jnp.exp(alog_ref[...]) * softplus
    lane = lax.broadcasted_iota(jnp.int32, ab.shape, 1)
    gb_ref[...] = jnp.where(lane < B_HEADS, gval, jnp.where(lane < 2 * B_HEADS, jax.nn.sigmoid(ab), 0.0))


def _proj(h, pre, ws, tabs, alog_row, dt_row, *, tm):
    t, d = h.shape
    wq, wk, wv, wc, wz, wab, wga, wgb = ws
    cos, slo, shi = tabs
    nt = cos.shape[0] // tm
    row = lambda w: pl.BlockSpec((tm, w), lambda i: (i, 0))
    tab = pl.BlockSpec((tm, HEAD_W), lambda i: (i % nt, 0))
    sd = lambda w, dt: jax.ShapeDtypeStruct((t, w), dt)
    cw = wc.shape[1]
    return pl.pallas_call(
        _proj_kernel,
        grid=(t // tm,),
        in_specs=[row(d), _const_spec((1, d))] + [_const_spec(w.shape) for w in ws]
                 + [tab, tab, tab, _const_spec((1, HEAD_W)), _const_spec((1, HEAD_W))],
        out_specs=[row(d), row(d), row(d), row(d), row(d), row(cw), row(d), row(HEAD_W), row(d), row(d)],
        out_shape=[sd(d, BF16), sd(d, F32), sd(d, BF16), sd(d, F32), sd(d, BF16), sd(cw, F32), sd(d, BF16),
                   sd(HEAD_W, F32), sd(d, BF16), sd(d, BF16)],
        compiler_params=_cparams(("parallel",)),
        name="proj",
    )(h, pre, *ws, cos, slo, shi, alog_row, dt_row)


def _lambda_full(lam_ref, lam_init):
    l = lam_ref[...]
    s1 = jnp.sum(l[0:1] * l[1:2], axis=-1, keepdims=True)
    s2 = jnp.sum(l[2:3] * l[3:4], axis=-1, keepdims=True)
    return jnp.exp(s1) - jnp.exp(s2) + lam_init


def _attn_kernel(qi_ref, ki_ref, lam_ref, sub_ref, qt_ref, k_ref, vt_ref, o_ref,
                 q1_s, q2_s, m_s, l_s, acc_s, *, lam_init):
    p = pl.program_id(2)
    qi = qi_ref[p]
    ki = ki_ref[p]
    tq, tk = qt_ref.shape[3], k_ref.shape[1]

    @pl.when(ki == 0)
    def _init():
        qt = qt_ref[0, 0].astype(F32)
        row = lax.broadcasted_iota(jnp.int32, qt.shape, 0)
        q1_s[...] = jnp.where(row < A_HD, qt, 0.0).astype(BF16)
        q2_s[...] = jnp.where(row >= A_HD, qt, 0.0).astype(BF16)
        m_s[...] = jnp.full(m_s.shape, NEG_BIG, F32)
        l_s[...] = jnp.zeros(l_s.shape, F32)
        acc_s[...] = jnp.zeros(acc_s.shape, F32)

    def _step(masked):
        k = k_ref[0]
        vt = vt_ref[0, 0]
        if masked:
            krow = lax.broadcasted_iota(jnp.int32, (tk, tq), 0)
            qcol = lax.broadcasted_iota(jnp.int32, (tk, tq), 1)
            keep = krow <= qcol
        for m, qs in enumerate((q1_s, q2_s)):
            s = _dot(k, qs[...])
            if masked:
                s = jnp.where(keep, s, NEG_BIG)
            m_prev = m_s[m]
            m_new = jnp.maximum(m_prev, jnp.max(s, axis=0, keepdims=True))
            alpha = jnp.exp2(m_prev - m_new)
            pm = jnp.exp2(s - m_new)
            l_s[m] = alpha * l_s[m] + jnp.sum(pm, axis=0, keepdims=True)
            acc_s[m] = alpha * acc_s[m] + _dot(vt, pm.astype(BF16))
            m_s[m] = m_new

    @pl.when(ki < qi)
    def _off():
        _step(False)

    @pl.when(ki == qi)
    def _diag():
        _step(True)
        lam = _lambda_full(lam_ref, lam_init)
        ot = acc_s[0] / l_s[0] - lam * (acc_s[1] / l_s[1])
        inv = lax.rsqrt(jnp.mean(ot * ot, axis=0, keepdims=True) + NORM_EPS)
        on = ot * inv * (sub_ref[...] * (1.0 - lam_init))
        o_ref[0] = on.T.astype(o_ref.dtype)


def _attn_prompt(qt, k, vt, lamv, subln_col, *, lam_init, tq):
    b, l, d = k.shape
    nq = l // tq
    pairs = [(i, j) for i in range(nq) for j in range(i + 1)]
    qi_tab = jnp.asarray(np.array([pq for pq, _ in pairs], np.int32))
    ki_tab = jnp.asarray(np.array([pk for _, pk in pairs], np.int32))
    qt_spec = pl.BlockSpec((1, 1, HEAD_W, tq), lambda bi, h, p, qt_, kt_: (bi, h, 0, qt_[p]))
    vt_spec = pl.BlockSpec((1, 1, HEAD_W, tq), lambda bi, h, p, qt_, kt_: (bi, h, 0, kt_[p]))
    k_spec = pl.BlockSpec((1, tq, HEAD_W), lambda bi, h, p, qt_, kt_: (bi, kt_[p], h))
    o_spec = pl.BlockSpec((1, tq, HEAD_W), lambda bi, h, p, qt_, kt_: (bi, qt_[p], h))
    grid_spec = pltpu.PrefetchScalarGridSpec(
        num_scalar_prefetch=2,
        grid=(b, A_HEADS, len(pairs)),
        in_specs=[
            pl.BlockSpec((4, A_HD), lambda bi, h, p, qt_, kt_: (0, 0)),
            pl.BlockSpec((HEAD_W, 1), lambda bi, h, p, qt_, kt_: (0, 0)),
            qt_spec, k_spec, vt_spec,
        ],
        out_specs=o_spec,
        scratch_shapes=[
            pltpu.VMEM((HEAD_W, tq), BF16), pltpu.VMEM((HEAD_W, tq), BF16),
            pltpu.VMEM((2, 1, tq), F32), pltpu.VMEM((2, 1, tq), F32), pltpu.VMEM((2, HEAD_W, tq), F32),
        ],
    )
    return pl.pallas_call(
        functools.partial(_attn_kernel, lam_init=lam_init),
        grid_spec=grid_spec,
        out_shape=jax.ShapeDtypeStruct((b, l, d), BF16),
        compiler_params=_cparams(("parallel", "parallel", "arbitrary")),
        name="attn_prompt",
    )(qi_tab, ki_tab, lamv, subln_col, qt, k, vt)


N_QROWS = 8
NEW_TOK_PAD = 16


PAGES_PER_STEP = 8


def _attn_decode_kernel(pt_ref, lam_ref, sub_ref, q_ref, kn_ref, vn_ref, *rest, lam_init, n_new, n_pg):
    kc_refs, vc_refs = rest[:n_pg], rest[n_pg:2 * n_pg]
    o_ref, m_s, l_s, acc_s, bias_s = rest[2 * n_pg:]
    pg = pl.program_id(1)
    qall = q_ref[0]
    nr = qall.shape[0]

    def _head_bias(ncols, causal):
        row = lax.broadcasted_iota(jnp.int32, (nr, ncols), 0)
        col = lax.broadcasted_iota(jnp.int32, (nr, ncols), 1)
        keep = (col % A_HEADS) == (row // N_QROWS)
        if causal:
            keep = jnp.logical_and(keep, (col // A_HEADS) <= (row % n_new))
        return jnp.where(keep, 0.0, NEG_BIG).astype(F32)

    def _update(ks, vs, biases):
        ss = [_dot_nt(qall, k) + bias for k, bias in zip(ks, biases)]
        m_prev = m_s[...]
        m_new = m_prev
        for s in ss:
            m_new = jnp.maximum(m_new, jnp.max(s, axis=-1, keepdims=True))
        alpha = jnp.exp2(m_prev - m_new)
        l = alpha * l_s[...]
        acc = alpha * acc_s[...]
        for s, v in zip(ss, vs):
            pm = jnp.exp2(s - m_new)
            l = l + jnp.sum(pm, axis=-1, keepdims=True)
            acc = acc + _dot(pm.astype(BF16), v)
        l_s[...] = l
        acc_s[...] = acc
        m_s[...] = m_new

    rows = kc_refs[0].shape[1] * kc_refs[0].shape[2]

    @pl.when(pg == 0)
    def _new_rows():
        m_s[...] = jnp.full(m_s.shape, NEG_BIG, F32)
        l_s[...] = jnp.zeros(l_s.shape, F32)
        acc_s[...] = jnp.zeros(acc_s.shape, F32)
        bias_s[...] = _head_bias(rows, False)
        kn = kn_ref[0]
        _update([kn], [vn_ref[0]], [_head_bias(kn.shape[0], True)])

    flat = lambda r: r[0].reshape(rows, HEAD_W).astype(BF16)
    bias = bias_s[...]
    _update([flat(r) for r in kc_refs], [flat(r) for r in vc_refs], [bias] * n_pg)

    @pl.when(pg == pl.num_programs(1) - 1)
    def _fin():
        lam = _lambda_full(lam_ref, lam_init)
        nrm = acc_s[...] / l_s[...]
        o = nrm - lam * pltpu.roll(nrm, nr - N_QROWS // 2, axis=0)
        o_ref[0] = (_rms(o, sub_ref[...]) * (1.0 - lam_init)).astype(o_ref.dtype)


def _attn_decode(qall, kn, vn, cache_k, cache_v, page_table, lamv, subln, *, lam_init, n_new):
    bs, n_pages = page_table.shape
    page = cache_k.shape[1]
    nr = qall.shape[1]
    n_pg = math.gcd(PAGES_PER_STEP, n_pages)
    seq = lambda r: pl.BlockSpec((1, r, HEAD_W), lambda b, p, pt: (b, 0, 0))
    cache = [pl.BlockSpec((1, page, A_HEADS, HEAD_W), lambda b, p, pt, j=j: (pt[b, p * n_pg + j], 0, 0, 0))
             for j in range(n_pg)]
    grid_spec = pltpu.PrefetchScalarGridSpec(
        num_scalar_prefetch=1,
        grid=(bs, n_pages // n_pg),
        in_specs=[
            pl.BlockSpec((4, A_HD), lambda b, p, pt: (0, 0)),
            pl.BlockSpec((1, HEAD_W), lambda b, p, pt: (0, 0)),
            seq(nr), seq(kn.shape[1]), seq(vn.shape[1]),
        ] + cache + cache,
        out_specs=seq(nr),
        scratch_shapes=[pltpu.VMEM((nr, 1), F32), pltpu.VMEM((nr, 1), F32), pltpu.VMEM((nr, HEAD_W), F32),
                        pltpu.VMEM((nr, page * A_HEADS), F32)],
    )
    return pl.pallas_call(
        functools.partial(_attn_decode_kernel, lam_init=lam_init, n_new=n_new, n_pg=n_pg),
        grid_spec=grid_spec,
        out_shape=jax.ShapeDtypeStruct((bs, nr, HEAD_W), BF16),
        compiler_params=_cparams(("parallel", "arbitrary")),
        name="attn_decode",
    )(page_table, lamv, subln, qall, kn, vn, *([cache_k] * n_pg), *([cache_v] * n_pg))


CONV_PAD = 8


def _delta_kernel(x_ref, conv0_ref, gb_ref, z_ref, s0_ref, cw_ref, nw_ref, o_ref, sout_ref, xp_s, st_s):
    j = pl.program_id(1)
    c = x_ref.shape[1]
    nh = st_s.shape[0]
    dk = st_s.shape[1]
    kw = nh * dk

    @pl.when(j == 0)
    def _init():
        xp_s[0:CONV_PAD, :] = conv0_ref[0]
        st_s[...] = s0_ref[0]

    xp_s[CONV_PAD:CONV_PAD + c, :] = x_ref[0]

    def conv(col):
        acc = None
        for i in range(CONV_W):
            r0 = CONV_PAD - (CONV_W - 1) + i
            term = xp_s[r0:r0 + c, col:col + dk] * cw_ref[i:i + 1, col:col + dk]
            acc = term if acc is None else acc + term
        return acc * jax.nn.sigmoid(acc)

    def l2n(x):
        return x * lax.rsqrt(jnp.sum(x * x, axis=-1, keepdims=True) + NORM_EPS)

    gb = gb_ref[0]
    ri = lax.broadcasted_iota(jnp.int32, (c, c), 0)
    ci = lax.broadcasted_iota(jnp.int32, (c, c), 1)
    incl = ri >= ci
    strict = ri > ci
    ltri = jnp.where(incl, 1.0, 0.0).astype(F32)
    eye = jnp.where(ri == ci, 1.0, 0.0).astype(F32)
    gc_all = _dot(ltri, gb, HIGHEST)
    n_double = int(math.log2(c)) - 1

    for h in range(nh):
        q = l2n(conv(h * dk)) * (dk ** -0.5)
        k = l2n(conv(kw + h * dk))
        v = conv(2 * kw + h * dk)
        g_col = gb[:, h:h + 1]
        beta = gb[:, nh + h:nh + h + 1]
        gc = gc_all[:, h:h + 1]
        gc_last = gc_all[c - 1:c, h:h + 1]
        gd = _dot(ltri, jnp.where(strict, g_col, 0.0), HIGHEST)
        decay = jnp.where(incl, jnp.exp(gd), 0.0)
        kbeta = k * beta
        a = jnp.where(strict, _dot_nt(kbeta, k, HIGHEST) * decay, 0.0)
        tinv = eye - a
        apow = a
        for _ in range(n_double):
            apow = _dot(apow, apow, HIGHEST)
            tinv = tinv + _dot(tinv, apow, HIGHEST)
        e_gc = jnp.exp(gc)
        u = _dot(tinv, v * beta, HIGHEST)
        w = _dot(tinv, kbeta * e_gc, HIGHEST)
        qk = _dot_nt(q, k, HIGHEST) * decay
        q_dec = q * e_gc
        k_dec = k * jnp.exp(gc_last - gc)
        st = st_s[h]
        st_b = st.astype(BF16)
        v_new = u - _dot(w.astype(BF16), st_b)
        v_new_b = v_new.astype(BF16)
        o = _dot(q_dec.astype(BF16), st_b) + _dot(qk.astype(BF16), v_new_b)
        st_s[h] = st * jnp.exp(gc_last) + _dot_tn(k_dec.astype(BF16), v_new_b)
        zh = z_ref[0, :, h * dk:(h + 1) * dk].astype(F32)
        o_ref[0, :, h * dk:(h + 1) * dk] = (_rms(o, nw_ref[...]) * (zh * jax.nn.sigmoid(zh))).astype(o_ref.dtype)

    xp_s[0:CONV_PAD, :] = xp_s[c:c + CONV_PAD, :]

    @pl.when(j == pl.num_programs(1) - 1)
    def _fin():
        sout_ref[0] = st_s[...]


def _delta(x, conv0, gb, z, s0, conv_w, norm_w, *, chunk):
    b, l, cc = x.shape
    nh, dk, dv = s0.shape[1:]
    blk = lambda w: pl.BlockSpec((1, chunk, w), lambda bi, j: (bi, j, 0))
    return pl.pallas_call(
        _delta_kernel,
        grid=(b, l // chunk),
        in_specs=[
            blk(cc),
            pl.BlockSpec((1, CONV_PAD, cc), lambda bi, j: (bi, 0, 0)),
            blk(gb.shape[2]), blk(z.shape[2]),
            pl.BlockSpec((1, nh, dk, dv), lambda bi, j: (bi, 0, 0, 0)),
            pl.BlockSpec((CONV_W, cc), lambda bi, j: (0, 0)),
            pl.BlockSpec((1, dv), lambda bi, j: (0, 0)),
        ],
        out_specs=[blk(nh * dv), pl.BlockSpec((1, nh, dk, dv), lambda bi, j: (bi, 0, 0, 0))],
        out_shape=[jax.ShapeDtypeStruct((b, l, nh * dv), BF16), jax.ShapeDtypeStruct(s0.shape, F32)],
        scratch_shapes=[pltpu.VMEM((chunk + CONV_PAD, cc), F32), pltpu.VMEM((nh, dk, dv), F32)],
        compiler_params=_cparams(("parallel", "arbitrary")),
        name="delta",
    )(x, conv0, gb, z, s0, conv_w, norm_w)


def _merge_kernel(h_ref, oa_ref, ob_ref, ga_ref, gb_ref, wa_ref, wb_ref, wo_ref, post_ref, o_ref):
    a = _dot(oa_ref[...], wa_ref[...])
    b = _dot(ob_ref[...], wb_ref[...])
    m = jax.nn.sigmoid(ga_ref[...].astype(F32)) * a + jax.nn.sigmoid(gb_ref[...].astype(F32)) * b
    r = _dot(m.astype(BF16), wo_ref[...])
    o_ref[...] = h_ref[...] + _rms(r, post_ref[...])


def _merge(h, oa, ob, ga, gb, wa, wb, wo, post, *, tm):
    t, d = h.shape
    row = pl.BlockSpec((tm, d), lambda i: (i, 0))
    return pl.pallas_call(
        _merge_kernel,
        grid=(t // tm,),
        in_specs=[row, row, row, row, row, _const_spec(wa.shape), _const_spec(wb.shape), _const_spec(wo.shape),
                  _const_spec((1, d))],
        out_specs=row,
        out_shape=jax.ShapeDtypeStruct((t, d), F32),
        compiler_params=_cparams(("parallel",)),
        name="merge",
    )(h, oa, ob, ga, gb, wa, wb, wo, post)


def _rope_tables(pos):
    half = A_HD // 2
    inv = jnp.power(ROPE_THETA, -2.0 * jnp.arange(half, dtype=F32) / A_HD)
    ang = pos.astype(F32)[:, None] * inv[None, :]
    cos, sin = jnp.cos(ang), jnp.sin(ang)
    zero = jnp.zeros_like(sin)
    reps = HEAD_W // A_HD
    cos_t = jnp.tile(jnp.concatenate([cos, cos], axis=1), (1, reps))
    sin_lo = jnp.tile(jnp.concatenate([-sin, zero], axis=1), (1, reps))
    sin_hi = jnp.tile(jnp.concatenate([zero, sin], axis=1), (1, reps))
    return cos_t, sin_lo, sin_hi


def _pad_rows(x, rows, front=False):
    pad = rows - x.shape[1]
    cfg = [(0, 0)] * x.ndim
    cfg[1] = (pad, 0) if front else (0, pad)
    return jnp.pad(x, cfg)


def _layer(x, tabs, p, lam_init, *, tm, past):
    b, l, d = x.shape
    t = b * l
    h1 = _ffn(x.reshape(t, d), p['ffn1_pre'], p['ffn1_wg'], p['ffn1_wu'], p['ffn1_wo'], p['ffn1_post'], tm=tm)
    q, kf, kb, vf, vb, c, z, gb, ga, gbt = _proj(h1, p['mix_pre'], p['w_in'], tabs, p['alog'], p['dt'], tm=tm)
    cc = c.shape[1]
    c3 = c.reshape(b, l, cc)
    new_conv = c3[:, l - (CONV_W - 1):]
    if past is None:
        heads_t = lambda a: a.reshape(b, l, A_HEADS, HEAD_W).transpose(0, 2, 3, 1)
        o_a = _attn_prompt(heads_t(q), kb.reshape(b, l, d), heads_t(vb), p['lamv'], p['subln'].reshape(-1, 1),
                           lam_init=lam_init, tq=min(512, l))
        conv0 = jnp.zeros((b, CONV_PAD, cc), F32)
        s0 = jnp.zeros((b, B_HEADS, B_DK, B_DK), F32)
        o_b, new_s = _delta(c3, conv0, gb.reshape(b, l, -1), z.reshape(b, l, d), s0, p['conv_w'], p['dnorm'],
                            chunk=DN_CHUNK)
        o_a = o_a.reshape(t, d)
        o_b = o_b.reshape(t, d)
    else:
        cache_k, cache_v, page_table, conv_state, s_state = past
        assert 2 * l == N_QROWS
        qh = q.reshape(b, l, A_HEADS, HEAD_W).transpose(0, 2, 1, 3)[:, :, None]
        lane_map = (jnp.arange(HEAD_W) // A_HD).reshape(1, 1, 1, 1, HEAD_W)
        qall = jnp.where(lane_map == jnp.arange(2).reshape(1, 1, 2, 1, 1), qh, jnp.zeros((), q.dtype))
        qall = qall.reshape(b, A_HEADS * N_QROWS, HEAD_W)
        new_rows = lambda a: _pad_rows(a.reshape(b, l, A_HEADS, HEAD_W), NEW_TOK_PAD).reshape(b, -1, HEAD_W)
        o_a = _attn_decode(qall, new_rows(kb), new_rows(vb), cache_k, cache_v, page_table, p['lamv'], p['subln'],
                           lam_init=lam_init, n_new=l)
        o_a = o_a.reshape(b, A_HEADS, N_QROWS, HEAD_W)[:, :, :l].transpose(0, 2, 1, 3).reshape(t, d)
        lp = DN_CHUNK
        o_b, new_s = _delta(_pad_rows(c3, lp), _pad_rows(conv_state, CONV_PAD, front=True),
                            _pad_rows(gb.reshape(b, l, -1), lp), _pad_rows(z.reshape(b, l, d), lp),
                            s_state, p['conv_w'], p['dnorm'], chunk=DN_CHUNK)
        o_b = o_b[:, :l].reshape(t, d)
    h2 = _merge(h1, o_a, o_b, ga, gbt, p['w_a'], p['w_b'], p['w_o'], p['mix_post'], tm=tm)
    y = _ffn(h2, p['ffn2_pre'], p['ffn2_wg'], p['ffn2_wu'], p['ffn2_wo'], p['ffn2_post'], tm=tm)
    k_rows = kf.reshape(b, l, A_HEADS, HEAD_W)
    v_rows = vf.reshape(b, l, A_HEADS, HEAD_W)
    return y.reshape(b, l, d), k_rows, v_rows, new_conv, new_s


def kernel(x_prompt, x_sample, cache_k, cache_v, state_conv, state_delta, page_table, ffn1_pre_norm, ffn1_w_in, ffn1_w_out, ffn1_post_norm, mix_pre_norm, w_in, conv_w, lambda_q1, lambda_k1, lambda_q2, lambda_k2, attn_subln, a_log, dt_bias, delta_norm, w_a_out, w_b_out, w_o, mix_post_norm, ffn2_pre_norm, ffn2_w_in, ffn2_w_out, ffn2_post_norm):
    depth = w_in.shape[0]
    d = x_prompt.shape[-1]
    lp, ls = x_prompt.shape[1], x_sample.shape[1]
    n_pages = page_table.shape[1]
    page = cache_k.shape[2]
    past_len = n_pages * page
    tabs_p = _rope_tables(jnp.arange(lp, dtype=jnp.int32))
    tabs_s1 = _rope_tables(past_len + jnp.arange(ls, dtype=jnp.int32))
    tabs_s = tuple(jnp.tile(tb, (x_sample.shape[0], 1)) for tb in tabs_s1)
    kw = B_HEADS * B_DK
    splits = np.cumsum([d, d, d, 3 * kw, kw, B_HEADS, B_HEADS, d, d])[:-1].tolist()
    yp, ys = x_prompt, x_sample
    outs = [[] for _ in range(8)]
    for i in range(depth):
        lam_init = 0.8 - 0.6 * math.exp(-0.3 * i)
        wq, wk, wv, wc, wz, wa_, wb_, wga, wgb = jnp.split(w_in[i], splits, axis=1)
        wab = jnp.pad(jnp.concatenate([wa_, wb_], axis=1), ((0, 0), (0, HEAD_W - 2 * B_HEADS)))
        row = lambda v: v.reshape(1, -1).astype(F32)
        lane_pad = lambda v: jnp.pad(v.reshape(1, -1).astype(F32), ((0, 0), (0, HEAD_W - v.shape[-1])))
        f = ffn1_w_in.shape[2] // 2
        p = {
            'ffn1_pre': row(ffn1_pre_norm[i]), 'ffn1_post': row(ffn1_post_norm[i]),
            'ffn1_wg': ffn1_w_in[i, :, :f].astype(BF16), 'ffn1_wu': ffn1_w_in[i, :, f:].astype(BF16),
            'ffn1_wo': ffn1_w_out[i].astype(BF16),
            'ffn2_pre': row(ffn2_pre_norm[i]), 'ffn2_post': row(ffn2_post_norm[i]),
            'ffn2_wg': ffn2_w_in[i, :, :f].astype(BF16), 'ffn2_wu': ffn2_w_in[i, :, f:].astype(BF16),
            'ffn2_wo': ffn2_w_out[i].astype(BF16),
            'mix_pre': row(mix_pre_norm[i]), 'mix_post': row(mix_post_norm[i]),
            'w_in': tuple(w.astype(BF16) for w in (wq, wk, wv, wc, wz, wab, wga, wgb)),
            'alog': lane_pad(a_log[i]), 'dt': lane_pad(dt_bias[i]),
            'lamv': jnp.stack([lambda_q1[i], lambda_k1[i], lambda_q2[i], lambda_k2[i]]).astype(F32),
            'subln': row(attn_subln[i]), 'dnorm': row(delta_norm[i]),
            'conv_w': conv_w[i].astype(F32),
            'w_a': w_a_out[i].astype(BF16), 'w_b': w_b_out[i].astype(BF16), 'w_o': w_o[i].astype(BF16),
        }
        yp, kp, vp, cp, sp = _layer(yp, tabs_p, p, lam_init, tm=256, past=None)
        ys, ks_, vs_, cs_, ss_ = _layer(ys, tabs_s, p, lam_init, tm=x_sample.shape[0] * ls,
                                        past=(cache_k[i], cache_v[i], page_table, state_conv[i], state_delta[i]))
        for lst, val in zip(outs, (kp, vp, cp, sp, ks_, vs_, cs_, ss_)):
            lst.append(val)
    return (yp, ys) + tuple(jnp.stack(o) for o in outs)
```

```python
import functools
import math

import numpy as np
import jax
import jax.numpy as jnp
from jax import lax
from jax.experimental import pallas as pl
from jax.experimental.pallas import tpu as pltpu

F32 = jnp.float32
BF16 = jnp.bfloat16

NORM_EPS = 1e-6
ROPE_THETA = 10000.0
A_HEADS = 8
A_HD = 64
HEAD_W = 2 * A_HD
B_HEADS = 8
B_DK = 128
CONV_W = 4
DN_CHUNK = 64
NEG_BIG = -1e30
ATTN_TQ = 1024
ATTN_TK = 1024
VT_ROWS = 2 * A_HD + 16
LOG2E = math.log2(math.e)
VMEM_LIMIT_BYTES = 56 * 1024 * 1024
HIGHEST = lax.Precision.HIGHEST


def _cparams(sem):
    return pltpu.CompilerParams(dimension_semantics=sem, vmem_limit_bytes=VMEM_LIMIT_BYTES)


def _const_spec(shape):
    zeros = (0,) * len(shape)
    return pl.BlockSpec(shape, lambda *_: zeros, pipeline_mode=pl.Buffered(1))


def _rms(x, w):
    return x * lax.rsqrt(jnp.mean(x * x, axis=-1, keepdims=True) + NORM_EPS) * w


def _dot(a, b, precision=None):
    return jnp.dot(a, b, preferred_element_type=F32, precision=precision)


def _dot_nt(a, b, precision=None):
    return lax.dot_general(a, b, (((1,), (1,)), ((), ())), preferred_element_type=F32, precision=precision)


def _dot_tn(a, b, precision=None):
    return lax.dot_general(a, b, (((0,), (0,)), ((), ())), preferred_element_type=F32, precision=precision)


def _ffn_kernel(x_ref, pre_ref, wg_ref, wu_ref, wo_ref, post_ref, o_ref, *, n_chunk):
    x = x_ref[...]
    xn = _rms(x, pre_ref[...]).astype(BF16)
    ck = wg_ref.shape[1] // n_chunk
    acc = None
    for c in range(n_chunk):
        g = _dot(xn, wg_ref[:, c * ck:(c + 1) * ck])
        u = _dot(xn, wu_ref[:, c * ck:(c + 1) * ck])
        a = (g * jax.nn.sigmoid(g) * u).astype(BF16)
        part = _dot(a, wo_ref[c * ck:(c + 1) * ck, :])
        acc = part if acc is None else acc + part
    o_ref[...] = x + 0.5 * _rms(acc, post_ref[...])


def _ffn(x, pre, wg, wu, wo, post, *, tm):
    t, d = x.shape
    f = wg.shape[1]
    n_chunk = 2 if f % 256 == 0 else 1
    return pl.pallas_call(
        functools.partial(_ffn_kernel, n_chunk=n_chunk),
        grid=(t // tm,),
        in_specs=[
            pl.BlockSpec((tm, d), lambda i: (i, 0)),
            _const_spec((1, d)), _const_spec((d, f)), _const_spec((d, f)), _const_spec((f, d)),
            _const_spec((1, d)),
        ],
        out_specs=pl.BlockSpec((tm, d), lambda i: (i, 0)),
        out_shape=jax.ShapeDtypeStruct((t, d), F32),
        compiler_params=_cparams(("parallel",)),
        name="ffn",
    )(x, pre, wg, wu, wo, post)


def _rope_heads(x, cos, sin_lo, sin_hi):
    outs = []
    for h in range(x.shape[1] // HEAD_W):
        xh = x[:, h * HEAD_W:(h + 1) * HEAD_W]
        up = pltpu.roll(xh, HEAD_W - A_HD // 2, axis=1)
        dn = pltpu.roll(xh, A_HD // 2, axis=1)
        outs.append(xh * cos + up * sin_lo + dn * sin_hi)
    return outs


def _proj_kernel(h_ref, pre_ref, wq_ref, wk_ref, wv_ref, wc_ref, wz_ref, wab_ref, wga_ref, wgb_ref,
                 cos_ref, slo_ref, shi_ref, alog_ref, dt_ref,
                 q_ref, kf_ref, kb_ref, vf_ref, vb_ref, c_ref, z_ref, gb_ref, ga_ref, gbt_ref):
    u = _rms(h_ref[...], pre_ref[...]).astype(BF16)
    cos, slo, shi = cos_ref[...], slo_ref[...], shi_ref[...]
    q_scale = (A_HD ** -0.5) * LOG2E
    for h, qh in enumerate(_rope_heads(_dot(u, wq_ref[...]), cos, slo, shi)):
        q_ref[:, h * HEAD_W:(h + 1) * HEAD_W] = (qh * q_scale).astype(BF16)
    for h, kh in enumerate(_rope_heads(_dot(u, wk_ref[...]), cos, slo, shi)):
        kf_ref[:, h * HEAD_W:(h + 1) * HEAD_W] = kh
        kb_ref[:, h * HEAD_W:(h + 1) * HEAD_W] = kh.astype(BF16)
    v = _dot(u, wv_ref[...])
    vf_ref[...] = v
    vb_ref[...] = v.astype(BF16)
    n_c = wc_ref.shape[1] // wq_ref.shape[1]
    for j in range(n_c):
        w = wq_ref.shape[1]
        c_ref[:, j * w:(j + 1) * w] = _dot(u, wc_ref[:, j * w:(j + 1) * w])
    z_ref[...] = _dot(u, wz_ref[...]).astype(BF16)
    ga_ref[...] = _dot(u, wga_ref[...]).astype(BF16)
    gbt_ref[...] = _dot(u, wgb_ref[...]).astype(BF16)
    ab = _dot(u, wab_ref[...])
    xs = ab + dt_ref[...]
    softplus = jnp.maximum(xs, 0.0) + jnp.log(1.0 + jnp.exp(-jnp.abs(xs)))
    gval = -jnp.exp(alog_ref[...]) * softplus
    lane = lax.broadcasted_iota(jnp.int32, ab.shape, 1)
    gb_ref[...] = jnp.where(lane < B_HEADS, gval, jnp.where(lane < 2 * B_HEADS, jax.nn.sigmoid(ab), 0.0))


def _proj(h, pre, ws, tabs, alog_row, dt_row, *, tm):
    t, d = h.shape
    wq, wk, wv, wc, wz, wab, wga, wgb = ws
    cos, slo, shi = tabs
    nt = cos.shape[0] // tm
    row = lambda w: pl.BlockSpec((tm, w), lambda i: (i, 0))
    tab = pl.BlockSpec((tm, HEAD_W), lambda i: (i % nt, 0))
    sd = lambda w, dt: jax.ShapeDtypeStruct((t, w), dt)
    cw = wc.shape[1]
    return pl.pallas_call(
        _proj_kernel,
        grid=(t // tm,),
        in_specs=[row(d), _const_spec((1, d))] + [_const_spec(w.shape) for w in ws]
                 + [tab, tab, tab, _const_spec((1, HEAD_W)), _const_spec((1, HEAD_W))],
        out_specs=[row(d), row(d), row(d), row(d), row(d), row(cw), row(d), row(HEAD_W), row(d), row(d)],
        out_shape=[sd(d, BF16), sd(d, F32), sd(d, BF16), sd(d, F32), sd(d, BF16), sd(cw, F32), sd(d, BF16),
                   sd(HEAD_W, F32), sd(d, BF16), sd(d, BF16)],
        compiler_params=_cparams(("parallel",)),
        name="proj",
    )(h, pre, *ws, cos, slo, shi, alog_row, dt_row)


def _lambda_full(lam_ref, lam_init):
    l = lam_ref[...]
    s1 = jnp.sum(l[0:1] * l[1:2], axis=-1, keepdims=True)
    s2 = jnp.sum(l[2:3] * l[3:4], axis=-1, keepdims=True)
    return jnp.exp(s1) - jnp.exp(s2) + lam_init


def _attn_kernel(qi_ref, ki_ref, lam_ref, sub_ref, qt_ref, k_ref, vt_ref, o_ref,
                 q1_s, q2_s, m_s, acc_s, *, lam_init):
    p = pl.program_id(2)
    ki = ki_ref[p]
    tq, tk = qt_ref.shape[3], k_ref.shape[1]
    off = ki * tk - qi_ref[p] * tq

    @pl.when(ki == 0)
    def _init():
        qt = qt_ref[0, 0].astype(F32)
        row = lax.broadcasted_iota(jnp.int32, qt.shape, 0)
        q1_s[...] = jnp.where(row < A_HD, qt, 0.0).astype(BF16)
        q2_s[...] = jnp.where(row >= A_HD, qt, 0.0).astype(BF16)
        m_s[...] = jnp.full(m_s.shape, NEG_BIG, F32)
        acc_s[...] = jnp.zeros(acc_s.shape, F32)

    def _step(masked):
        k = k_ref[0]
        vt = vt_ref[0, 0]
        if masked:
            krow = lax.broadcasted_iota(jnp.int32, (tk, tq), 0) + off
            qcol = lax.broadcasted_iota(jnp.int32, (tk, tq), 1)
            keep = krow <= qcol
        scores = [_dot(k, qs[...]) for qs in (q1_s, q2_s)]
        probs = []
        for m, s in enumerate(scores):
            if masked:
                s = jnp.where(keep, s, NEG_BIG)
            m_prev = m_s[m]
            m_new = jnp.maximum(m_prev, jnp.max(s, axis=0, keepdims=True))
            probs.append((jnp.exp2(m_prev - m_new), jnp.exp2((s - m_new).astype(BF16))))
            m_s[m] = m_new
        for m, (alpha, pm) in enumerate(probs):
            acc_s[m] = alpha * acc_s[m] + _dot(vt, pm)

    @pl.when(off < 0)
    def _below_diagonal():
        _step(False)

    @pl.when(off >= 0)
    def _on_diagonal():
        _step(True)

    @pl.when(off == tq - tk)
    def _finish():
        lam = _lambda_full(lam_ref, lam_init)
        n0, n1 = acc_s[0], acc_s[1]
        ot = n0[:HEAD_W] / n0[HEAD_W:HEAD_W + 1] - lam * (n1[:HEAD_W] / n1[HEAD_W:HEAD_W + 1])
        inv = lax.rsqrt(jnp.mean(ot * ot, axis=0, keepdims=True) + NORM_EPS)
        on = ot * inv * (sub_ref[...] * (1.0 - lam_init))
        o_ref[0] = on.T.astype(o_ref.dtype)


def _attn_prompt(qt, k, vt, lamv, subln_col, *, lam_init, tq, tk):
    b, l, d = k.shape
    assert tq % tk == 0 and l % tq == 0
    pairs = [(i, j) for i in range(l // tq) for j in range((i + 1) * (tq // tk))]
    qi_tab = jnp.asarray(np.array([pq for pq, _ in pairs], np.int32))
    ki_tab = jnp.asarray(np.array([pk for _, pk in pairs], np.int32))
    qt_spec = pl.BlockSpec((1, 1, HEAD_W, tq), lambda bi, h, p, qt_, kt_: (bi, h, 0, qt_[p]))
    vt_spec = pl.BlockSpec((1, 1, VT_ROWS, tk), lambda bi, h, p, qt_, kt_: (bi, h, 0, kt_[p]))
    k_spec = pl.BlockSpec((1, tk, HEAD_W), lambda bi, h, p, qt_, kt_: (bi, kt_[p], h))
    o_spec = pl.BlockSpec((1, tq, HEAD_W), lambda bi, h, p, qt_, kt_: (bi, qt_[p], h))
    grid_spec = pltpu.PrefetchScalarGridSpec(
        num_scalar_prefetch=2,
        grid=(b, A_HEADS, len(pairs)),
        in_specs=[
            pl.BlockSpec((4, A_HD), lambda bi, h, p, qt_, kt_: (0, 0)),
            pl.BlockSpec((HEAD_W, 1), lambda bi, h, p, qt_, kt_: (0, 0)),
            qt_spec, k_spec, vt_spec,
        ],
        out_specs=o_spec,
        scratch_shapes=[
            pltpu.VMEM((HEAD_W, tq), BF16), pltpu.VMEM((HEAD_W, tq), BF16),
            pltpu.VMEM((2, 1, tq), F32), pltpu.VMEM((2, VT_ROWS, tq), F32),
        ],
    )
    return pl.pallas_call(
        functools.partial(_attn_kernel, lam_init=lam_init),
        grid_spec=grid_spec,
        out_shape=jax.ShapeDtypeStruct((b, l, d), BF16),
        compiler_params=_cparams(("parallel", "parallel", "arbitrary")),
        name="attn_prompt",
    )(qi_tab, ki_tab, lamv, subln_col, qt, k, vt)


N_QROWS = 8
NEW_TOK_PAD = 16


PAGES_PER_STEP = 8


def _attn_decode_kernel(pt_ref, lam_ref, sub_ref, q_ref, kn_ref, vn_ref, *rest, lam_init, n_new, n_pg):
    kc_refs, vc_refs = rest[:n_pg], rest[n_pg:2 * n_pg]
    o_ref, m_s, l_s, acc_s, bias_s = rest[2 * n_pg:]
    pg = pl.program_id(1)
    qall = q_ref[0]
    nr = qall.shape[0]

    def _head_bias(ncols, causal):
        row = lax.broadcasted_iota(jnp.int32, (nr, ncols), 0)
        col = lax.broadcasted_iota(jnp.int32, (nr, ncols), 1)
        keep = (col % A_HEADS) == (row // N_QROWS)
        if causal:
            keep = jnp.logical_and(keep, (col // A_HEADS) <= (row % n_new))
        return jnp.where(keep, 0.0, NEG_BIG).astype(F32)

    def _update(ks, vs, biases):
        ss = [_dot_nt(qall, k) + bias for k, bias in zip(ks, biases)]
        m_prev = m_s[...]
        m_new = m_prev
        for s in ss:
            m_new = jnp.maximum(m_new, jnp.max(s, axis=-1, keepdims=True))
        alpha = jnp.exp2(m_prev - m_new)
        l = alpha * l_s[...]
        acc = alpha * acc_s[...]
        for s, v in zip(ss, vs):
            pm = jnp.exp2(s - m_new)
            l = l + jnp.sum(pm, axis=-1, keepdims=True)
            acc = acc + _dot(pm.astype(BF16), v)
        l_s[...] = l
        acc_s[...] = acc
        m_s[...] = m_new

    rows = kc_refs[0].shape[1] * kc_refs[0].shape[2]

    @pl.when(pg == 0)
    def _new_rows():
        m_s[...] = jnp.full(m_s.shape, NEG_BIG, F32)
        l_s[...] = jnp.zeros(l_s.shape, F32)
        acc_s[...] = jnp.zeros(acc_s.shape, F32)
        bias_s[...] = _head_bias(rows, False)
        kn = kn_ref[0]
        _update([kn], [vn_ref[0]], [_head_bias(kn.shape[0], True)])

    flat = lambda r: r[0].reshape(rows, HEAD_W).astype(BF16)
    bias = bias_s[...]
    _update([flat(r) for r in kc_refs], [flat(r) for r in vc_refs], [bias] * n_pg)

    @pl.when(pg == pl.num_programs(1) - 1)
    def _fin():
        lam = _lambda_full(lam_ref, lam_init)
        nrm = acc_s[...] / l_s[...]
        o = nrm - lam * pltpu.roll(nrm, nr - N_QROWS // 2, axis=0)
        o_ref[0] = (_rms(o, sub_ref[...]) * (1.0 - lam_init)).astype(o_ref.dtype)


def _attn_decode(qall, kn, vn, cache_k, cache_v, page_table, lamv, subln, *, lam_init, n_new):
    bs, n_pages = page_table.shape
    page = cache_k.shape[1]
    nr = qall.shape[1]
    n_pg = math.gcd(PAGES_PER_STEP, n_pages)
    seq = lambda r: pl.BlockSpec((1, r, HEAD_W), lambda b, p, pt: (b, 0, 0))
    cache = [pl.BlockSpec((1, page, A_HEADS, HEAD_W), lambda b, p, pt, j=j: (pt[b, p * n_pg + j], 0, 0, 0))
             for j in range(n_pg)]
    grid_spec = pltpu.PrefetchScalarGridSpec(
        num_scalar_prefetch=1,
        grid=(bs, n_pages // n_pg),
        in_specs=[
            pl.BlockSpec((4, A_HD), lambda b, p, pt: (0, 0)),
            pl.BlockSpec((1, HEAD_W), lambda b, p, pt: (0, 0)),
            seq(nr), seq(kn.shape[1]), seq(vn.shape[1]),
        ] + cache + cache,
        out_specs=seq(nr),
        scratch_shapes=[pltpu.VMEM((nr, 1), F32), pltpu.VMEM((nr, 1), F32), pltpu.VMEM((nr, HEAD_W), F32),
                        pltpu.VMEM((nr, page * A_HEADS), F32)],
    )
    return pl.pallas_call(
        functools.partial(_attn_decode_kernel, lam_init=lam_init, n_new=n_new, n_pg=n_pg),
        grid_spec=grid_spec,
        out_shape=jax.ShapeDtypeStruct((bs, nr, HEAD_W), BF16),
        compiler_params=_cparams(("parallel", "arbitrary")),
        name="attn_decode",
    )(page_table, lamv, subln, qall, kn, vn, *([cache_k] * n_pg), *([cache_v] * n_pg))


CONV_PAD = 8
MXU_DEPTH = 256
SAMPLE_CHUNK = 16


def _split(x):
    hi = x.astype(BF16)
    return hi, (x - hi.astype(F32)).astype(BF16)


def _dot3(a, b):
    return _dot(a[0], b[0]) + (_dot(a[0], b[1]) + _dot(a[1], b[0]))


def _dot3_nt(a, b):
    (ah, al), (bh, bl) = _split(a), _split(b)
    return _dot_nt(ah, bh) + (_dot_nt(ah, bl) + _dot_nt(al, bh))


def _delta_kernel(x_ref, conv0_ref, gb_ref, z_ref, s0_ref, cw_ref, nw_ref, o_ref, sout_ref, xp_s, st_s):
    j = pl.program_id(1)
    c = x_ref.shape[1]
    nh = st_s.shape[0]
    dk = st_s.shape[1]
    kw = nh * dk

    @pl.when(j == 0)
    def _init():
        xp_s[0:CONV_PAD, :] = conv0_ref[0]
        st_s[...] = s0_ref[0]

    xp_s[CONV_PAD:CONV_PAD + c, :] = x_ref[0]

    def conv(col):
        acc = None
        for i in range(CONV_W):
            r0 = CONV_PAD - (CONV_W - 1) + i
            term = xp_s[r0:r0 + c, col:col + dk] * cw_ref[i:i + 1, col:col + dk]
            acc = term if acc is None else acc + term
        return acc * jax.nn.sigmoid(acc)

    def l2n(x):
        return x * lax.rsqrt(jnp.sum(x * x, axis=-1, keepdims=True) + NORM_EPS)

    gb = gb_ref[0]
    r1 = lax.broadcasted_iota(jnp.int32, (c, c), 0)
    c1 = lax.broadcasted_iota(jnp.int32, (c, c), 1)
    ltri = jnp.where(r1 >= c1, 1.0, 0.0).astype(F32)
    gc_all = _dot(ltri, gb, HIGHEST)

    grp = max(1, min(nh, MXU_DEPTH // c))
    n = grp * c
    shift = int(math.log2(c))
    ri = lax.broadcasted_iota(jnp.int32, (n, n), 0)
    ci = lax.broadcasted_iota(jnp.int32, (n, n), 1)
    same = (ri >> shift) == (ci >> shift)
    incl = jnp.logical_and(same, ri >= ci)
    strict = jnp.logical_and(same, ri > ci)
    diag = ri == ci
    eye = jnp.where(diag, 1.0, 0.0).astype(F32)
    n_double = shift - 1
    stack = lambda parts: jnp.concatenate(parts, axis=0) if len(parts) > 1 else parts[0]

    for g0 in range(0, nh, grp):
        heads = range(g0, g0 + grp)
        q = stack([l2n(conv(h * dk)) * (dk ** -0.5) for h in heads])
        k = stack([l2n(conv(kw + h * dk)) for h in heads])
        v = stack([conv(2 * kw + h * dk) for h in heads])
        beta = stack([gb[:, nh + h:nh + h + 1] for h in heads])
        gc = stack([gc_all[:, h:h + 1] for h in heads])
        gc_last = stack([jnp.broadcast_to(gc_all[c - 1:c, h:h + 1], (c, 1)) for h in heads])
        gc_row = jnp.sum(jnp.where(diag, gc, 0.0), axis=0, keepdims=True)
        decay = jnp.exp(jnp.where(incl, gc - gc_row, NEG_BIG))
        kbeta = k * beta
        a = jnp.where(strict, _dot3_nt(kbeta, k) * decay, 0.0)
        tinv = eye - a
        apow = _split(a)
        for _ in range(n_double):
            apow = _split(_dot3(apow, apow))
            tinv = tinv + _dot3(_split(tinv), apow)
        e_gc = jnp.exp(gc)
        uw = _dot3(_split(tinv), _split(jnp.concatenate([v * beta, kbeta * e_gc], axis=1)))
        u, w = uw[:, :dk], uw[:, dk:]
        qk = (_dot_nt(q.astype(BF16), k.astype(BF16)) * decay).astype(BF16)
        q_dec = (q * e_gc).astype(BF16)
        k_dec = (k * jnp.exp(gc_last - gc)).astype(BF16)
        w_b = w.astype(BF16)
        qs, v_new = [], []
        for i, h in enumerate(heads):
            rows = slice(i * c, (i + 1) * c)
            wq = _dot(jnp.concatenate([w_b[rows], q_dec[rows]], axis=0), st_s[h].astype(BF16))
            v_new.append(u[rows] - wq[:c])
            qs.append(wq[c:])
        v_new_b = stack(v_new).astype(BF16)
        o = stack(qs) + _dot(qk, v_new_b)
        for i, h in enumerate(heads):
            rows = slice(i * c, (i + 1) * c)
            st_s[h] = st_s[h] * jnp.exp(gc_all[c - 1:c, h:h + 1]) + _dot_tn(k_dec[rows], v_new_b[rows])
            zh = z_ref[0, :, h * dk:(h + 1) * dk].astype(F32)
            o_ref[0, :, h * dk:(h + 1) * dk] = (
                _rms(o[rows], nw_ref[...]) * (zh * jax.nn.sigmoid(zh))).astype(o_ref.dtype)

    xp_s[0:CONV_PAD, :] = xp_s[c:c + CONV_PAD, :]

    @pl.when(j == pl.num_programs(1) - 1)
    def _fin():
        sout_ref[0] = st_s[...]


def _delta(x, conv0, gb, z, s0, conv_w, norm_w, *, chunk):
    b, l, cc = x.shape
    nh, dk, dv = s0.shape[1:]
    blk = lambda w: pl.BlockSpec((1, chunk, w), lambda bi, j: (bi, j, 0))
    return pl.pallas_call(
        _delta_kernel,
        grid=(b, l // chunk),
        in_specs=[
            blk(cc),
            pl.BlockSpec((1, CONV_PAD, cc), lambda bi, j: (bi, 0, 0)),
            blk(gb.shape[2]), blk(z.shape[2]),
            pl.BlockSpec((1, nh, dk, dv), lambda bi, j: (bi, 0, 0, 0)),
            pl.BlockSpec((CONV_W, cc), lambda bi, j: (0, 0)),
            pl.BlockSpec((1, dv), lambda bi, j: (0, 0)),
        ],
        out_specs=[blk(nh * dv), pl.BlockSpec((1, nh, dk, dv), lambda bi, j: (bi, 0, 0, 0))],
        out_shape=[jax.ShapeDtypeStruct((b, l, nh * dv), BF16), jax.ShapeDtypeStruct(s0.shape, F32)],
        scratch_shapes=[pltpu.VMEM((chunk + CONV_PAD, cc), F32), pltpu.VMEM((nh, dk, dv), F32)],
        compiler_params=_cparams(("parallel", "arbitrary")),
        name="delta",
    )(x, conv0, gb, z, s0, conv_w, norm_w)


def _merge_kernel(h_ref, oa_ref, ob_ref, ga_ref, gb_ref, wa_ref, wb_ref, wo_ref, post_ref, o_ref):
    a = _dot(oa_ref[...], wa_ref[...])
    b = _dot(ob_ref[...], wb_ref[...])
    m = jax.nn.sigmoid(ga_ref[...].astype(F32)) * a + jax.nn.sigmoid(gb_ref[...].astype(F32)) * b
    r = _dot(m.astype(BF16), wo_ref[...])
    o_ref[...] = h_ref[...] + _rms(r, post_ref[...])


def _merge(h, oa, ob, ga, gb, wa, wb, wo, post, *, tm):
    t, d = h.shape
    row = pl.BlockSpec((tm, d), lambda i: (i, 0))
    return pl.pallas_call(
        _merge_kernel,
        grid=(t // tm,),
        in_specs=[row, row, row, row, row, _const_spec(wa.shape), _const_spec(wb.shape), _const_spec(wo.shape),
                  _const_spec((1, d))],
        out_specs=row,
        out_shape=jax.ShapeDtypeStruct((t, d), F32),
        compiler_params=_cparams(("parallel",)),
        name="merge",
    )(h, oa, ob, ga, gb, wa, wb, wo, post)


def _rope_tables(pos):
    half = A_HD // 2
    inv = jnp.power(ROPE_THETA, -2.0 * jnp.arange(half, dtype=F32) / A_HD)
    ang = pos.astype(F32)[:, None] * inv[None, :]
    cos, sin = jnp.cos(ang), jnp.sin(ang)
    zero = jnp.zeros_like(sin)
    reps = HEAD_W // A_HD
    cos_t = jnp.tile(jnp.concatenate([cos, cos], axis=1), (1, reps))
    sin_lo = jnp.tile(jnp.concatenate([-sin, zero], axis=1), (1, reps))
    sin_hi = jnp.tile(jnp.concatenate([zero, sin], axis=1), (1, reps))
    return cos_t, sin_lo, sin_hi


def _pad_rows(x, rows, front=False):
    pad = rows - x.shape[1]
    cfg = [(0, 0)] * x.ndim
    cfg[1] = (pad, 0) if front else (0, pad)
    return jnp.pad(x, cfg)


def _layer(x, tabs, p, lam_init, *, tm, past):
    b, l, d = x.shape
    t = b * l
    h1 = _ffn(x.reshape(t, d), p['ffn1_pre'], p['ffn1_wg'], p['ffn1_wu'], p['ffn1_wo'], p['ffn1_post'], tm=tm)
    q, kf, kb, vf, vb, c, z, gb, ga, gbt = _proj(h1, p['mix_pre'], p['w_in'], tabs, p['alog'], p['dt'], tm=tm)
    cc = c.shape[1]
    c3 = c.reshape(b, l, cc)
    new_conv = c3[:, l - (CONV_W - 1):]
    if past is None:
        heads_t = lambda a: a.reshape(b, l, A_HEADS, HEAD_W).transpose(0, 2, 3, 1)
        ones_rows = jnp.zeros((b, A_HEADS, VT_ROWS - HEAD_W, l), BF16).at[:, :, 0].set(1.0)
        vt = jnp.concatenate([heads_t(vb), ones_rows], axis=2)
        o_a = _attn_prompt(heads_t(q), kb.reshape(b, l, d), vt, p['lamv'], p['subln'].reshape(-1, 1),
                           lam_init=lam_init, tq=min(ATTN_TQ, l), tk=min(ATTN_TK, l))
        conv0 = jnp.zeros((b, CONV_PAD, cc), F32)
        s0 = jnp.zeros((b, B_HEADS, B_DK, B_DK), F32)
        o_b, new_s = _delta(c3, conv0, gb.reshape(b, l, -1), z.reshape(b, l, d), s0, p['conv_w'], p['dnorm'],
                            chunk=DN_CHUNK)
        o_a = o_a.reshape(t, d)
        o_b = o_b.reshape(t, d)
    else:
        cache_k, cache_v, page_table, conv_state, s_state = past
        assert 2 * l == N_QROWS
        qh = q.reshape(b, l, A_HEADS, HEAD_W).transpose(0, 2, 1, 3)[:, :, None]
        lane_map = (jnp.arange(HEAD_W) // A_HD).reshape(1, 1, 1, 1, HEAD_W)
        qall = jnp.where(lane_map == jnp.arange(2).reshape(1, 1, 2, 1, 1), qh, jnp.zeros((), q.dtype))
        qall = qall.reshape(b, A_HEADS * N_QROWS, HEAD_W)
        new_rows = lambda a: _pad_rows(a.reshape(b, l, A_HEADS, HEAD_W), NEW_TOK_PAD).reshape(b, -1, HEAD_W)
        o_a = _attn_decode(qall, new_rows(kb), new_rows(vb), cache_k, cache_v, page_table, p['lamv'], p['subln'],
                           lam_init=lam_init, n_new=l)
        o_a = o_a.reshape(b, A_HEADS, N_QROWS, HEAD_W)[:, :, :l].transpose(0, 2, 1, 3).reshape(t, d)
        lp = -(-l // SAMPLE_CHUNK) * SAMPLE_CHUNK
        o_b, new_s = _delta(_pad_rows(c3, lp), _pad_rows(conv_state, CONV_PAD, front=True),
                            _pad_rows(gb.reshape(b, l, -1), lp), _pad_rows(z.reshape(b, l, d), lp),
                            s_state, p['conv_w'], p['dnorm'], chunk=SAMPLE_CHUNK)
        o_b = o_b[:, :l].reshape(t, d)
    h2 = _merge(h1, o_a, o_b, ga, gbt, p['w_a'], p['w_b'], p['w_o'], p['mix_post'], tm=tm)
    y = _ffn(h2, p['ffn2_pre'], p['ffn2_wg'], p['ffn2_wu'], p['ffn2_wo'], p['ffn2_post'], tm=tm)
    k_rows = kf.reshape(b, l, A_HEADS, HEAD_W)
    v_rows = vf.reshape(b, l, A_HEADS, HEAD_W)
    return y.reshape(b, l, d), k_rows, v_rows, new_conv, new_s


def kernel(x_prompt, x_sample, cache_k, cache_v, state_conv, state_delta, page_table, ffn1_pre_norm, ffn1_w_in, ffn1_w_out, ffn1_post_norm, mix_pre_norm, w_in, conv_w, lambda_q1, lambda_k1, lambda_q2, lambda_k2, attn_subln, a_log, dt_bias, delta_norm, w_a_out, w_b_out, w_o, mix_post_norm, ffn2_pre_norm, ffn2_w_in, ffn2_w_out, ffn2_post_norm):
    depth = w_in.shape[0]
    d = x_prompt.shape[-1]
    lp, ls = x_prompt.shape[1], x_sample.shape[1]
    n_pages = page_table.shape[1]
    page = cache_k.shape[2]
    past_len = n_pages * page
    tabs_p = _rope_tables(jnp.arange(lp, dtype=jnp.int32))
    tabs_s1 = _rope_tables(past_len + jnp.arange(ls, dtype=jnp.int32))
    tabs_s = tuple(jnp.tile(tb, (x_sample.shape[0], 1)) for tb in tabs_s1)
    kw = B_HEADS * B_DK
    splits = np.cumsum([d, d, d, 3 * kw, kw, B_HEADS, B_HEADS, d, d])[:-1].tolist()
    yp, ys = x_prompt, x_sample
    outs = [[] for _ in range(8)]
    for i in range(depth):
        lam_init = 0.8 - 0.6 * math.exp(-0.3 * i)
        wq, wk, wv, wc, wz, wa_, wb_, wga, wgb = jnp.split(w_in[i], splits, axis=1)
        wab = jnp.pad(jnp.concatenate([wa_, wb_], axis=1), ((0, 0), (0, HEAD_W - 2 * B_HEADS)))
        row = lambda v: v.reshape(1, -1).astype(F32)
        lane_pad = lambda v: jnp.pad(v.reshape(1, -1).astype(F32), ((0, 0), (0, HEAD_W - v.shape[-1])))
        f = ffn1_w_in.shape[2] // 2
        p = {
            'ffn1_pre': row(ffn1_pre_norm[i]), 'ffn1_post': row(ffn1_post_norm[i]),
            'ffn1_wg': ffn1_w_in[i, :, :f].astype(BF16), 'ffn1_wu': ffn1_w_in[i, :, f:].astype(BF16),
            'ffn1_wo': ffn1_w_out[i].astype(BF16),
            'ffn2_pre': row(ffn2_pre_norm[i]), 'ffn2_post': row(ffn2_post_norm[i]),
            'ffn2_wg': ffn2_w_in[i, :, :f].astype(BF16), 'ffn2_wu': ffn2_w_in[i, :, f:].astype(BF16),
            'ffn2_wo': ffn2_w_out[i].astype(BF16),
            'mix_pre': row(mix_pre_norm[i]), 'mix_post': row(mix_post_norm[i]),
            'w_in': tuple(w.astype(BF16) for w in (wq, wk, wv, wc, wz, wab, wga, wgb)),
            'alog': lane_pad(a_log[i]), 'dt': lane_pad(dt_bias[i]),
            'lamv': jnp.stack([lambda_q1[i], lambda_k1[i], lambda_q2[i], lambda_k2[i]]).astype(F32),
            'subln': row(attn_subln[i]), 'dnorm': row(delta_norm[i]),
            'conv_w': conv_w[i].astype(F32),
            'w_a': w_a_out[i].astype(BF16), 'w_b': w_b_out[i].astype(BF16), 'w_o': w_o[i].astype(BF16),
        }
        yp, kp, vp, cp, sp = _layer(yp, tabs_p, p, lam_init, tm=256, past=None)
        ys, ks_, vs_, cs_, ss_ = _layer(ys, tabs_s, p, lam_init, tm=x_sample.shape[0] * ls,
                                        past=(cache_k[i], cache_v[i], page_table, state_conv[i], state_delta[i]))
        for lst, val in zip(outs, (kp, vp, cp, sp, ks_, vs_, cs_, ss_)):
            lst.append(val)
    return (yp, ys) + tuple(jnp.stack(o) for o in outs)
```

```python
import functools
import math

import numpy as np
import jax
import jax.numpy as jnp
from jax import lax
from jax.experimental import pallas as pl
from jax.experimental.pallas import tpu as pltpu

F32 = jnp.float32
BF16 = jnp.bfloat16

NORM_EPS = 1e-6
ROPE_THETA = 10000.0
A_HEADS = 8
A_HD = 64
HEAD_W = 2 * A_HD
B_HEADS = 8
B_DK = 128
CONV_W = 4
DN_CHUNK = 64
NEG_BIG = -1e30
ATTN_TQ = 1024
ATTN_TK = 1024
VT_ROWS = 2 * A_HD + 16
LOG2E = math.log2(math.e)
VMEM_LIMIT_BYTES = 56 * 1024 * 1024
HIGHEST = lax.Precision.HIGHEST


def _cparams(sem):
    return pltpu.CompilerParams(dimension_semantics=sem, vmem_limit_bytes=VMEM_LIMIT_BYTES)


def _const_spec(shape):
    zeros = (0,) * len(shape)
    return pl.BlockSpec(shape, lambda *_: zeros, pipeline_mode=pl.Buffered(1))


def _rms(x, w):
    return x * lax.rsqrt(jnp.mean(x * x, axis=-1, keepdims=True) + NORM_EPS) * w


def _dot(a, b, precision=None):
    return jnp.dot(a, b, preferred_element_type=F32, precision=precision)


def _dot_nt(a, b, precision=None):
    return lax.dot_general(a, b, (((1,), (1,)), ((), ())), preferred_element_type=F32, precision=precision)


def _dot_tn(a, b, precision=None):
    return lax.dot_general(a, b, (((0,), (0,)), ((), ())), preferred_element_type=F32, precision=precision)


def _ffn_kernel(x_ref, pre_ref, wg_ref, wu_ref, wo_ref, post_ref, o_ref, *, n_chunk):
    x = x_ref[...]
    xn = _rms(x, pre_ref[...]).astype(BF16)
    ck = wg_ref.shape[1] // n_chunk
    acc = None
    for c in range(n_chunk):
        g = _dot(xn, wg_ref[:, c * ck:(c + 1) * ck])
        u = _dot(xn, wu_ref[:, c * ck:(c + 1) * ck])
        a = (g * jax.nn.sigmoid(g) * u).astype(BF16)
        part = _dot(a, wo_ref[c * ck:(c + 1) * ck, :])
        acc = part if acc is None else acc + part
    o_ref[...] = x + 0.5 * _rms(acc, post_ref[...])


def _ffn(x, pre, wg, wu, wo, post, *, tm):
    t, d = x.shape
    f = wg.shape[1]
    n_chunk = 2 if f % 256 == 0 else 1
    return pl.pallas_call(
        functools.partial(_ffn_kernel, n_chunk=n_chunk),
        grid=(t // tm,),
        in_specs=[
            pl.BlockSpec((tm, d), lambda i: (i, 0)),
            _const_spec((1, d)), _const_spec((d, f)), _const_spec((d, f)), _const_spec((f, d)),
            _const_spec((1, d)),
        ],
        out_specs=pl.BlockSpec((tm, d), lambda i: (i, 0)),
        out_shape=jax.ShapeDtypeStruct((t, d), F32),
        compiler_params=_cparams(("parallel",)),
        name="ffn",
    )(x, pre, wg, wu, wo, post)


def _rope_heads(x, cos, sin_lo, sin_hi):
    outs = []
    for h in range(x.shape[1] // HEAD_W):
        xh = x[:, h * HEAD_W:(h + 1) * HEAD_W]
        up = pltpu.roll(xh, HEAD_W - A_HD // 2, axis=1)
        dn = pltpu.roll(xh, A_HD // 2, axis=1)
        outs.append(xh * cos + up * sin_lo + dn * sin_hi)
    return outs


def _proj_kernel(h_ref, pre_ref, wq_ref, wk_ref, wv_ref, wc_ref, wz_ref, wab_ref, wga_ref, wgb_ref,
                 cos_ref, slo_ref, shi_ref, alog_ref, dt_ref,
                 q_ref, kf_ref, kb_ref, vf_ref, vb_ref, c_ref, z_ref, gb_ref, ga_ref, gbt_ref):
    u = _rms(h_ref[...], pre_ref[...]).astype(BF16)
    cos, slo, shi = cos_ref[...], slo_ref[...], shi_ref[...]
    q_scale = (A_HD ** -0.5) * LOG2E
    for h, qh in enumerate(_rope_heads(_dot(u, wq_ref[...]), cos, slo, shi)):
        q_ref[:, h * HEAD_W:(h + 1) * HEAD_W] = (qh * q_scale).astype(BF16)
    for h, kh in enumerate(_rope_heads(_dot(u, wk_ref[...]), cos, slo, shi)):
        kf_ref[:, h * HEAD_W:(h + 1) * HEAD_W] = kh
        kb_ref[:, h * HEAD_W:(h + 1) * HEAD_W] = kh.astype(BF16)
    v = _dot(u, wv_ref[...])
    vf_ref[...] = v
    vb_ref[...] = v.astype(BF16)
    n_c = wc_ref.shape[1] // wq_ref.shape[1]
    for j in range(n_c):
        w = wq_ref.shape[1]
        c_ref[:, j * w:(j + 1) * w] = _dot(u, wc_ref[:, j * w:(j + 1) * w])
    z_ref[...] = _dot(u, wz_ref[...]).astype(BF16)
    ga_ref[...] = _dot(u, wga_ref[...]).astype(BF16)
    gbt_ref[...] = _dot(u, wgb_ref[...]).astype(BF16)
    ab = _dot(u, wab_ref[...])
    xs = ab + dt_ref[...]
    softplus = jnp.maximum(xs, 0.0) + jnp.log(1.0 + jnp.exp(-jnp.abs(xs)))
    gval = -jnp.exp(alog_ref[...]) * softplus
    lane = lax.broadcasted_iota(jnp.int32, ab.shape, 1)
    gb_ref[...] = jnp.where(lane < B_HEADS, gval, jnp.where(lane < 2 * B_HEADS, jax.nn.sigmoid(ab), 0.0))


def _proj(h, pre, ws, tabs, alog_row, dt_row, *, tm):
    t, d = h.shape
    wq, wk, wv, wc, wz, wab, wga, wgb = ws
    cos, slo, shi = tabs
    nt = cos.shape[0] // tm
    row = lambda w: pl.BlockSpec((tm, w), lambda i: (i, 0))
    tab = pl.BlockSpec((tm, HEAD_W), lambda i: (i % nt, 0))
    sd = lambda w, dt: jax.ShapeDtypeStruct((t, w), dt)
    cw = wc.shape[1]
    return pl.pallas_call(
        _proj_kernel,
        grid=(t // tm,),
        in_specs=[row(d), _const_spec((1, d))] + [_const_spec(w.shape) for w in ws]
                 + [tab, tab, tab, _const_spec((1, HEAD_W)), _const_spec((1, HEAD_W))],
        out_specs=[row(d), row(d), row(d), row(d), row(d), row(cw), row(d), row(HEAD_W), row(d), row(d)],
        out_shape=[sd(d, BF16), sd(d, F32), sd(d, BF16), sd(d, F32), sd(d, BF16), sd(cw, F32), sd(d, BF16),
                   sd(HEAD_W, F32), sd(d, BF16), sd(d, BF16)],
        compiler_params=_cparams(("parallel",)),
        name="proj",
    )(h, pre, *ws, cos, slo, shi, alog_row, dt_row)


def _lambda_full(lam_ref, lam_init):
    l = lam_ref[...]
    s1 = jnp.sum(l[0:1] * l[1:2], axis=-1, keepdims=True)
    s2 = jnp.sum(l[2:3] * l[3:4], axis=-1, keepdims=True)
    return jnp.exp(s1) - jnp.exp(s2) + lam_init


def _attn_kernel(qi_ref, ki_ref, lam_ref, sub_ref, qt_ref, k_ref, vt_ref, o_ref,
                 q1_s, q2_s, m_s, acc_s, *, lam_init):
    p = pl.program_id(2)
    ki = ki_ref[p]
    tq, tk = qt_ref.shape[3], k_ref.shape[1]
    off = ki * tk - qi_ref[p] * tq

    @pl.when(ki == 0)
    def _init():
        qt = qt_ref[0, 0].astype(F32)
        row = lax.broadcasted_iota(jnp.int32, qt.shape, 0)
        q1_s[...] = jnp.where(row < A_HD, qt, 0.0).astype(BF16)
        q2_s[...] = jnp.where(row >= A_HD, qt, 0.0).astype(BF16)
        m_s[...] = jnp.full(m_s.shape, NEG_BIG, F32)
        acc_s[...] = jnp.zeros(acc_s.shape, F32)

    def _step(masked):
        k = k_ref[0]
        vt = vt_ref[0, 0]
        if masked:
            krow = lax.broadcasted_iota(jnp.int32, (tk, tq), 0) + off
            qcol = lax.broadcasted_iota(jnp.int32, (tk, tq), 1)
            keep = krow <= qcol
        scores = [_dot(k, qs[...]) for qs in (q1_s, q2_s)]
        probs = []
        for m, s in enumerate(scores):
            if masked:
                s = jnp.where(keep, s, NEG_BIG)
            m_prev = m_s[m]
            m_new = jnp.maximum(m_prev, jnp.max(s, axis=0, keepdims=True))
            probs.append((jnp.exp2(m_prev - m_new), jnp.exp2((s - m_new).astype(BF16))))
            m_s[m] = m_new
        for m, (alpha, pm) in enumerate(probs):
            acc_s[m] = alpha * acc_s[m] + _dot(vt, pm)

    @pl.when(off < 0)
    def _below_diagonal():
        _step(False)

    @pl.when(off >= 0)
    def _on_diagonal():
        _step(True)

    @pl.when(off == tq - tk)
    def _finish():
        lam = _lambda_full(lam_ref, lam_init)
        n0, n1 = acc_s[0], acc_s[1]
        ot = n0[:HEAD_W] / n0[HEAD_W:HEAD_W + 1] - lam * (n1[:HEAD_W] / n1[HEAD_W:HEAD_W + 1])
        inv = lax.rsqrt(jnp.mean(ot * ot, axis=0, keepdims=True) + NORM_EPS)
        on = ot * inv * (sub_ref[...] * (1.0 - lam_init))
        o_ref[0] = on.T.astype(o_ref.dtype)


def _attn_prompt(qt, k, vt, lamv, subln_col, *, lam_init, tq, tk):
    b, l, d = k.shape
    assert tq % tk == 0 and l % tq == 0
    pairs = [(i, j) for i in range(l // tq) for j in range((i + 1) * (tq // tk))]
    qi_tab = jnp.asarray(np.array([pq for pq, _ in pairs], np.int32))
    ki_tab = jnp.asarray(np.array([pk for _, pk in pairs], np.int32))
    qt_spec = pl.BlockSpec((1, 1, HEAD_W, tq), lambda bi, h, p, qt_, kt_: (bi, h, 0, qt_[p]))
    vt_spec = pl.BlockSpec((1, 1, VT_ROWS, tk), lambda bi, h, p, qt_, kt_: (bi, h, 0, kt_[p]))
    k_spec = pl.BlockSpec((1, tk, HEAD_W), lambda bi, h, p, qt_, kt_: (bi, kt_[p], h))
    o_spec = pl.BlockSpec((1, tq, HEAD_W), lambda bi, h, p, qt_, kt_: (bi, qt_[p], h))
    grid_spec = pltpu.PrefetchScalarGridSpec(
        num_scalar_prefetch=2,
        grid=(b, A_HEADS, len(pairs)),
        in_specs=[
            pl.BlockSpec((4, A_HD), lambda bi, h, p, qt_, kt_: (0, 0)),
            pl.BlockSpec((HEAD_W, 1), lambda bi, h, p, qt_, kt_: (0, 0)),
            qt_spec, k_spec, vt_spec,
        ],
        out_specs=o_spec,
        scratch_shapes=[
            pltpu.VMEM((HEAD_W, tq), BF16), pltpu.VMEM((HEAD_W, tq), BF16),
            pltpu.VMEM((2, 1, tq), F32), pltpu.VMEM((2, VT_ROWS, tq), F32),
        ],
    )
    return pl.pallas_call(
        functools.partial(_attn_kernel, lam_init=lam_init),
        grid_spec=grid_spec,
        out_shape=jax.ShapeDtypeStruct((b, l, d), BF16),
        compiler_params=_cparams(("parallel", "parallel", "arbitrary")),
        name="attn_prompt",
    )(qi_tab, ki_tab, lamv, subln_col, qt, k, vt)


N_QROWS = 8
NEW_TOK_PAD = 16


PAGES_PER_STEP = 16


def _attn_decode_kernel(pt_ref, lam_ref, sub_ref, q_ref, kn_ref, vn_ref, *rest, lam_init, n_new, n_pg):
    kc_refs, vc_refs = rest[:n_pg], rest[n_pg:2 * n_pg]
    o_ref, m_s, l_s, acc_s, bias_s = rest[2 * n_pg:]
    pg = pl.program_id(1)
    qall = q_ref[0]
    nr = qall.shape[0]

    def _head_bias(ncols, causal):
        row = lax.broadcasted_iota(jnp.int32, (nr, ncols), 0)
        col = lax.broadcasted_iota(jnp.int32, (nr, ncols), 1)
        keep = (col % A_HEADS) == (row // N_QROWS)
        if causal:
            keep = jnp.logical_and(keep, (col // A_HEADS) <= (row % n_new))
        return jnp.where(keep, 0.0, NEG_BIG).astype(F32)

    def _update(ks, vs, biases):
        ss = [_dot_nt(qall, k) + bias for k, bias in zip(ks, biases)]
        m_prev = m_s[...]
        m_new = m_prev
        for s in ss:
            m_new = jnp.maximum(m_new, jnp.max(s, axis=-1, keepdims=True))
        alpha = jnp.exp2(m_prev - m_new)
        l = alpha * l_s[...]
        acc = alpha * acc_s[...]
        for s, v in zip(ss, vs):
            pm = jnp.exp2(s - m_new)
            l = l + jnp.sum(pm, axis=-1, keepdims=True)
            acc = acc + _dot(pm.astype(BF16), v)
        l_s[...] = l
        acc_s[...] = acc
        m_s[...] = m_new

    rows = kc_refs[0].shape[1] * kc_refs[0].shape[2]

    @pl.when(pg == 0)
    def _new_rows():
        m_s[...] = jnp.full(m_s.shape, NEG_BIG, F32)
        l_s[...] = jnp.zeros(l_s.shape, F32)
        acc_s[...] = jnp.zeros(acc_s.shape, F32)
        bias_s[...] = _head_bias(rows, False)
        kn = kn_ref[0]
        _update([kn], [vn_ref[0]], [_head_bias(kn.shape[0], True)])

    flat = lambda r: r[0].reshape(rows, HEAD_W).astype(BF16)
    bias = bias_s[...]
    _update([flat(r) for r in kc_refs], [flat(r) for r in vc_refs], [bias] * n_pg)

    @pl.when(pg == pl.num_programs(1) - 1)
    def _fin():
        lam = _lambda_full(lam_ref, lam_init)
        nrm = acc_s[...] / l_s[...]
        o = nrm - lam * pltpu.roll(nrm, nr - N_QROWS // 2, axis=0)
        o_ref[0] = (_rms(o, sub_ref[...]) * (1.0 - lam_init)).astype(o_ref.dtype)


def _attn_decode(qall, kn, vn, cache_k, cache_v, page_table, lamv, subln, *, lam_init, n_new):
    bs, n_pages = page_table.shape
    page = cache_k.shape[1]
    nr = qall.shape[1]
    n_pg = math.gcd(PAGES_PER_STEP, n_pages)
    seq = lambda r: pl.BlockSpec((1, r, HEAD_W), lambda b, p, pt: (b, 0, 0))
    cache = [pl.BlockSpec((1, page, A_HEADS, HEAD_W), lambda b, p, pt, j=j: (pt[b, p * n_pg + j], 0, 0, 0))
             for j in range(n_pg)]
    grid_spec = pltpu.PrefetchScalarGridSpec(
        num_scalar_prefetch=1,
        grid=(bs, n_pages // n_pg),
        in_specs=[
            pl.BlockSpec((4, A_HD), lambda b, p, pt: (0, 0)),
            pl.BlockSpec((1, HEAD_W), lambda b, p, pt: (0, 0)),
            seq(nr), seq(kn.shape[1]), seq(vn.shape[1]),
        ] + cache + cache,
        out_specs=seq(nr),
        scratch_shapes=[pltpu.VMEM((nr, 1), F32), pltpu.VMEM((nr, 1), F32), pltpu.VMEM((nr, HEAD_W), F32),
                        pltpu.VMEM((nr, page * A_HEADS), F32)],
    )
    return pl.pallas_call(
        functools.partial(_attn_decode_kernel, lam_init=lam_init, n_new=n_new, n_pg=n_pg),
        grid_spec=grid_spec,
        out_shape=jax.ShapeDtypeStruct((bs, nr, HEAD_W), BF16),
        compiler_params=_cparams(("parallel", "arbitrary")),
        name="attn_decode",
    )(page_table, lamv, subln, qall, kn, vn, *([cache_k] * n_pg), *([cache_v] * n_pg))


CONV_PAD = 8
MXU_DEPTH = 256
SAMPLE_CHUNK = 16
DN_CHUNKS_PER_STEP = 4


def _delta_kernel(x_ref, conv0_ref, gb_ref, z_ref, s0_ref, cw_ref, nw_ref, o_ref, sout_ref, xp_s, st_s, *, c):
    j = pl.program_id(1)
    rows_step = x_ref.shape[1]
    nh = st_s.shape[0]
    dk = st_s.shape[1]
    kw = nh * dk

    @pl.when(j == 0)
    def _init():
        xp_s[0:CONV_PAD, :] = conv0_ref[0]
        st_s[...] = s0_ref[0]

    xp_s[CONV_PAD:CONV_PAD + rows_step, :] = x_ref[0]

    def conv(r0, col):
        acc = None
        for i in range(CONV_W):
            rr = r0 + CONV_PAD - (CONV_W - 1) + i
            term = xp_s[rr:rr + c, col:col + dk] * cw_ref[i:i + 1, col:col + dk]
            acc = term if acc is None else acc + term
        return acc * jax.nn.sigmoid(acc)

    def l2n(x):
        return x * lax.rsqrt(jnp.sum(x * x, axis=-1, keepdims=True) + NORM_EPS)

    r1 = lax.broadcasted_iota(jnp.int32, (c, c), 0)
    c1 = lax.broadcasted_iota(jnp.int32, (c, c), 1)
    ltri = jnp.where(r1 >= c1, 1.0, 0.0).astype(F32)

    grp = max(1, min(nh, MXU_DEPTH // c))
    n = grp * c
    shift = int(math.log2(c))
    ri = lax.broadcasted_iota(jnp.int32, (n, n), 0)
    ci = lax.broadcasted_iota(jnp.int32, (n, n), 1)
    same = (ri >> shift) == (ci >> shift)
    incl = jnp.logical_and(same, ri >= ci)
    strict = jnp.logical_and(same, ri > ci)
    diag = ri == ci
    eye = jnp.where(diag, 1.0, 0.0).astype(F32)
    n_double = shift - 1
    stack = lambda parts: jnp.concatenate(parts, axis=0) if len(parts) > 1 else parts[0]

    def chunk_local(r0):
        gb = gb_ref[0, r0:r0 + c, :]
        gc_all = _dot(ltri, gb, HIGHEST)
        groups = []
        for g0 in range(0, nh, grp):
            heads = range(g0, g0 + grp)
            q = stack([l2n(conv(r0, h * dk)) * (dk ** -0.5) for h in heads])
            k = stack([l2n(conv(r0, kw + h * dk)) for h in heads])
            v = stack([conv(r0, 2 * kw + h * dk) for h in heads])
            beta = stack([gb[:, nh + h:nh + h + 1] for h in heads])
            gc = stack([gc_all[:, h:h + 1] for h in heads])
            gc_last = stack([jnp.broadcast_to(gc_all[c - 1:c, h:h + 1], (c, 1)) for h in heads])
            gc_row = jnp.sum(jnp.where(diag, gc, 0.0), axis=0, keepdims=True)
            decay = jnp.exp(jnp.where(incl, gc - gc_row, NEG_BIG))
            kbeta = k * beta
            k_b = k.astype(BF16)
            a = jnp.where(strict, _dot_nt(kbeta.astype(BF16), k_b) * decay, 0.0)
            tinv = eye - a
            apow = a.astype(BF16)
            for _ in range(n_double):
                apow = _dot(apow, apow).astype(BF16)
                tinv = tinv + _dot(tinv.astype(BF16), apow)
            e_gc = jnp.exp(gc)
            uw = _dot(tinv.astype(BF16), jnp.concatenate([v * beta, kbeta * e_gc], axis=1).astype(BF16))
            groups.append(dict(
                heads=heads, u=uw[:, :dk], w=uw[:, dk:].astype(BF16),
                qk=(_dot_nt(q.astype(BF16), k_b) * decay).astype(BF16),
                q_dec=(q * e_gc).astype(BF16), k_dec=(k * jnp.exp(gc_last - gc)).astype(BF16),
                g_tot=[jnp.exp(gc_all[c - 1:c, h:h + 1]) for h in heads]))
        return groups

    def state_update(r0, groups):
        for grp_v in groups:
            heads = grp_v['heads']
            qs, v_new = [], []
            for i, h in enumerate(heads):
                rows = slice(i * c, (i + 1) * c)
                wq = _dot(jnp.concatenate([grp_v['w'][rows], grp_v['q_dec'][rows]], axis=0),
                          st_s[h].astype(BF16))
                v_new.append(grp_v['u'][rows] - wq[:c])
                qs.append(wq[c:])
            v_new_b = stack(v_new).astype(BF16)
            o = stack(qs) + _dot(grp_v['qk'], v_new_b)
            for i, h in enumerate(heads):
                rows = slice(i * c, (i + 1) * c)
                st_s[h] = st_s[h] * grp_v['g_tot'][i] + _dot_tn(grp_v['k_dec'][rows], v_new_b[rows])
                zh = z_ref[0, r0:r0 + c, h * dk:(h + 1) * dk].astype(F32)
                o_ref[0, r0:r0 + c, h * dk:(h + 1) * dk] = (
                    _rms(o[rows], nw_ref[...]) * (zh * jax.nn.sigmoid(zh))).astype(o_ref.dtype)

    starts = range(0, rows_step, c)
    local = [chunk_local(r0) for r0 in starts]
    for r0, groups in zip(starts, local):
        state_update(r0, groups)

    xp_s[0:CONV_PAD, :] = xp_s[rows_step:rows_step + CONV_PAD, :]

    @pl.when(j == pl.num_programs(1) - 1)
    def _fin():
        sout_ref[0] = st_s[...]


def _delta(x, conv0, gb, z, s0, conv_w, norm_w, *, chunk, chunks_per_step):
    b, l, cc = x.shape
    nh, dk, dv = s0.shape[1:]
    rows_step = chunk * chunks_per_step
    assert l % rows_step == 0
    blk = lambda w: pl.BlockSpec((1, rows_step, w), lambda bi, j: (bi, j, 0))
    return pl.pallas_call(
        functools.partial(_delta_kernel, c=chunk),
        grid=(b, l // rows_step),
        in_specs=[
            blk(cc),
            pl.BlockSpec((1, CONV_PAD, cc), lambda bi, j: (bi, 0, 0)),
            blk(gb.shape[2]), blk(z.shape[2]),
            pl.BlockSpec((1, nh, dk, dv), lambda bi, j: (bi, 0, 0, 0)),
            pl.BlockSpec((CONV_W, cc), lambda bi, j: (0, 0)),
            pl.BlockSpec((1, dv), lambda bi, j: (0, 0)),
        ],
        out_specs=[blk(nh * dv), pl.BlockSpec((1, nh, dk, dv), lambda bi, j: (bi, 0, 0, 0))],
        out_shape=[jax.ShapeDtypeStruct((b, l, nh * dv), BF16), jax.ShapeDtypeStruct(s0.shape, F32)],
        scratch_shapes=[pltpu.VMEM((rows_step + CONV_PAD, cc), F32), pltpu.VMEM((nh, dk, dv), F32)],
        compiler_params=_cparams(("parallel", "arbitrary")),
        name="delta",
    )(x, conv0, gb, z, s0, conv_w, norm_w)


def _merge_kernel(h_ref, oa_ref, ob_ref, ga_ref, gb_ref, wa_ref, wb_ref, wo_ref, post_ref, o_ref):
    a = _dot(oa_ref[...], wa_ref[...])
    b = _dot(ob_ref[...], wb_ref[...])
    m = jax.nn.sigmoid(ga_ref[...].astype(F32)) * a + jax.nn.sigmoid(gb_ref[...].astype(F32)) * b
    r = _dot(m.astype(BF16), wo_ref[...])
    o_ref[...] = h_ref[...] + _rms(r, post_ref[...])


def _merge(h, oa, ob, ga, gb, wa, wb, wo, post, *, tm):
    t, d = h.shape
    row = pl.BlockSpec((tm, d), lambda i: (i, 0))
    return pl.pallas_call(
        _merge_kernel,
        grid=(t // tm,),
        in_specs=[row, row, row, row, row, _const_spec(wa.shape), _const_spec(wb.shape), _const_spec(wo.shape),
                  _const_spec((1, d))],
        out_specs=row,
        out_shape=jax.ShapeDtypeStruct((t, d), F32),
        compiler_params=_cparams(("parallel",)),
        name="merge",
    )(h, oa, ob, ga, gb, wa, wb, wo, post)


def _rope_tables(pos):
    half = A_HD // 2
    inv = jnp.power(ROPE_THETA, -2.0 * jnp.arange(half, dtype=F32) / A_HD)
    ang = pos.astype(F32)[:, None] * inv[None, :]
    cos, sin = jnp.cos(ang), jnp.sin(ang)
    zero = jnp.zeros_like(sin)
    reps = HEAD_W // A_HD
    cos_t = jnp.tile(jnp.concatenate([cos, cos], axis=1), (1, reps))
    sin_lo = jnp.tile(jnp.concatenate([-sin, zero], axis=1), (1, reps))
    sin_hi = jnp.tile(jnp.concatenate([zero, sin], axis=1), (1, reps))
    return cos_t, sin_lo, sin_hi


def _pad_rows(x, rows, front=False):
    pad = rows - x.shape[1]
    cfg = [(0, 0)] * x.ndim
    cfg[1] = (pad, 0) if front else (0, pad)
    return jnp.pad(x, cfg)


def _layer(x, tabs, p, lam_init, *, tm, past):
    b, l, d = x.shape
    t = b * l
    h1 = _ffn(x.reshape(t, d), p['ffn1_pre'], p['ffn1_wg'], p['ffn1_wu'], p['ffn1_wo'], p['ffn1_post'], tm=tm)
    q, kf, kb, vf, vb, c, z, gb, ga, gbt = _proj(h1, p['mix_pre'], p['w_in'], tabs, p['alog'], p['dt'], tm=tm)
    cc = c.shape[1]
    c3 = c.reshape(b, l, cc)
    new_conv = c3[:, l - (CONV_W - 1):]
    if past is None:
        heads_t = lambda a: a.reshape(b, l, A_HEADS, HEAD_W).transpose(0, 2, 3, 1)
        ones_rows = jnp.zeros((b, A_HEADS, VT_ROWS - HEAD_W, l), BF16).at[:, :, 0].set(1.0)
        vt = jnp.concatenate([heads_t(vb), ones_rows], axis=2)
        o_a = _attn_prompt(heads_t(q), kb.reshape(b, l, d), vt, p['lamv'], p['subln'].reshape(-1, 1),
                           lam_init=lam_init, tq=min(ATTN_TQ, l), tk=min(ATTN_TK, l))
        conv0 = jnp.zeros((b, CONV_PAD, cc), F32)
        s0 = jnp.zeros((b, B_HEADS, B_DK, B_DK), F32)
        o_b, new_s = _delta(c3, conv0, gb.reshape(b, l, -1), z.reshape(b, l, d), s0, p['conv_w'], p['dnorm'],
                            chunk=DN_CHUNK, chunks_per_step=math.gcd(DN_CHUNKS_PER_STEP, l // DN_CHUNK))
        o_a = o_a.reshape(t, d)
        o_b = o_b.reshape(t, d)
    else:
        cache_k, cache_v, page_table, conv_state, s_state = past
        assert 2 * l == N_QROWS
        qh = q.reshape(b, l, A_HEADS, HEAD_W).transpose(0, 2, 1, 3)[:, :, None]
        lane_map = (jnp.arange(HEAD_W) // A_HD).reshape(1, 1, 1, 1, HEAD_W)
        qall = jnp.where(lane_map == jnp.arange(2).reshape(1, 1, 2, 1, 1), qh, jnp.zeros((), q.dtype))
        qall = qall.reshape(b, A_HEADS * N_QROWS, HEAD_W)
        new_rows = lambda a: _pad_rows(a.reshape(b, l, A_HEADS, HEAD_W), NEW_TOK_PAD).reshape(b, -1, HEAD_W)
        o_a = _attn_decode(qall, new_rows(kb), new_rows(vb), cache_k, cache_v, page_table, p['lamv'], p['subln'],
                           lam_init=lam_init, n_new=l)
        o_a = o_a.reshape(b, A_HEADS, N_QROWS, HEAD_W)[:, :, :l].transpose(0, 2, 1, 3).reshape(t, d)
        lp = -(-l // SAMPLE_CHUNK) * SAMPLE_CHUNK
        o_b, new_s = _delta(_pad_rows(c3, lp), _pad_rows(conv_state, CONV_PAD, front=True),
                            _pad_rows(gb.reshape(b, l, -1), lp), _pad_rows(z.reshape(b, l, d), lp),
                            s_state, p['conv_w'], p['dnorm'], chunk=SAMPLE_CHUNK, chunks_per_step=1)
        o_b = o_b[:, :l].reshape(t, d)
    h2 = _merge(h1, o_a, o_b, ga, gbt, p['w_a'], p['w_b'], p['w_o'], p['mix_post'], tm=tm)
    y = _ffn(h2, p['ffn2_pre'], p['ffn2_wg'], p['ffn2_wu'], p['ffn2_wo'], p['ffn2_post'], tm=tm)
    k_rows = kf.reshape(b, l, A_HEADS, HEAD_W)
    v_rows = vf.reshape(b, l, A_HEADS, HEAD_W)
    return y.reshape(b, l, d), k_rows, v_rows, new_conv, new_s


def kernel(x_prompt, x_sample, cache_k, cache_v, state_conv, state_delta, page_table, ffn1_pre_norm, ffn1_w_in, ffn1_w_out, ffn1_post_norm, mix_pre_norm, w_in, conv_w, lambda_q1, lambda_k1, lambda_q2, lambda_k2, attn_subln, a_log, dt_bias, delta_norm, w_a_out, w_b_out, w_o, mix_post_norm, ffn2_pre_norm, ffn2_w_in, ffn2_w_out, ffn2_post_norm):
    depth = w_in.shape[0]
    d = x_prompt.shape[-1]
    lp, ls = x_prompt.shape[1], x_sample.shape[1]
    n_pages = page_table.shape[1]
    page = cache_k.shape[2]
    past_len = n_pages * page
    tabs_p = _rope_tables(jnp.arange(lp, dtype=jnp.int32))
    tabs_s1 = _rope_tables(past_len + jnp.arange(ls, dtype=jnp.int32))
    tabs_s = tuple(jnp.tile(tb, (x_sample.shape[0], 1)) for tb in tabs_s1)
    kw = B_HEADS * B_DK
    splits = np.cumsum([d, d, d, 3 * kw, kw, B_HEADS, B_HEADS, d, d])[:-1].tolist()
    yp, ys = x_prompt, x_sample
    outs = [[] for _ in range(8)]
    for i in range(depth):
        lam_init = 0.8 - 0.6 * math.exp(-0.3 * i)
        wq, wk, wv, wc, wz, wa_, wb_, wga, wgb = jnp.split(w_in[i], splits, axis=1)
        wab = jnp.pad(jnp.concatenate([wa_, wb_], axis=1), ((0, 0), (0, HEAD_W - 2 * B_HEADS)))
        row = lambda v: v.reshape(1, -1).astype(F32)
        lane_pad = lambda v: jnp.pad(v.reshape(1, -1).astype(F32), ((0, 0), (0, HEAD_W - v.shape[-1])))
        f = ffn1_w_in.shape[2] // 2
        p = {
            'ffn1_pre': row(ffn1_pre_norm[i]), 'ffn1_post': row(ffn1_post_norm[i]),
            'ffn1_wg': ffn1_w_in[i, :, :f].astype(BF16), 'ffn1_wu': ffn1_w_in[i, :, f:].astype(BF16),
            'ffn1_wo': ffn1_w_out[i].astype(BF16),
            'ffn2_pre': row(ffn2_pre_norm[i]), 'ffn2_post': row(ffn2_post_norm[i]),
            'ffn2_wg': ffn2_w_in[i, :, :f].astype(BF16), 'ffn2_wu': ffn2_w_in[i, :, f:].astype(BF16),
            'ffn2_wo': ffn2_w_out[i].astype(BF16),
            'mix_pre': row(mix_pre_norm[i]), 'mix_post': row(mix_post_norm[i]),
            'w_in': tuple(w.astype(BF16) for w in (wq, wk, wv, wc, wz, wab, wga, wgb)),
            'alog': lane_pad(a_log[i]), 'dt': lane_pad(dt_bias[i]),
            'lamv': jnp.stack([lambda_q1[i], lambda_k1[i], lambda_q2[i], lambda_k2[i]]).astype(F32),
            'subln': row(attn_subln[i]), 'dnorm': row(delta_norm[i]),
            'conv_w': conv_w[i].astype(F32),
            'w_a': w_a_out[i].astype(BF16), 'w_b': w_b_out[i].astype(BF16), 'w_o': w_o[i].astype(BF16),
        }
        yp, kp, vp, cp, sp = _layer(yp, tabs_p, p, lam_init, tm=256, past=None)
        ys, ks_, vs_, cs_, ss_ = _layer(ys, tabs_s, p, lam_init, tm=x_sample.shape[0] * ls,
                                        past=(cache_k[i], cache_v[i], page_table, state_conv[i], state_delta[i]))
        for lst, val in zip(outs, (kp, vp, cp, sp, ks_, vs_, cs_, ss_)):
            lst.append(val)
    return (yp, ys) + tuple(jnp.stack(o) for o in outs)
```

```python
import functools
import math

import numpy as np
import jax
import jax.numpy as jnp
from jax import lax
from jax.experimental import pallas as pl
from jax.experimental.pallas import tpu as pltpu

F32 = jnp.float32
BF16 = jnp.bfloat16

NORM_EPS = 1e-6
ROPE_THETA = 10000.0
A_HEADS = 8
A_HD = 64
HEAD_W = 2 * A_HD
B_HEADS = 8
B_DK = 128
CONV_W = 4
DN_CHUNK = 64
NEG_BIG = -1e30
ATTN_TQ = 1024
ATTN_TK = 1024
VT_ROWS = 2 * A_HD + 16
LOG2E = math.log2(math.e)
VMEM_LIMIT_BYTES = 56 * 1024 * 1024
HIGHEST = lax.Precision.HIGHEST
PROMPT_TM = 512
PROJ_TM = 256


def _cparams(sem):
    return pltpu.CompilerParams(dimension_semantics=sem, vmem_limit_bytes=VMEM_LIMIT_BYTES)


def _const_spec(shape):
    zeros = (0,) * len(shape)
    return pl.BlockSpec(shape, lambda *_: zeros, pipeline_mode=pl.Buffered(1))


def _rms(x, w):
    return x * lax.rsqrt(jnp.mean(x * x, axis=-1, keepdims=True) + NORM_EPS) * w


def _dot(a, b, precision=None):
    return jnp.dot(a, b, preferred_element_type=F32, precision=precision)


def _dot_nt(a, b, precision=None):
    return lax.dot_general(a, b, (((1,), (1,)), ((), ())), preferred_element_type=F32, precision=precision)


def _dot_tn(a, b, precision=None):
    return lax.dot_general(a, b, (((0,), (0,)), ((), ())), preferred_element_type=F32, precision=precision)


def _ffn_rows(x, pre_ref, wg_ref, wu_ref, wo_ref, post_ref, n_chunk):
    xn = _rms(x, pre_ref[...]).astype(BF16)
    ck = wg_ref.shape[1] // n_chunk
    acc = None
    for c in range(n_chunk):
        g = _dot(xn, wg_ref[:, c * ck:(c + 1) * ck])
        u = _dot(xn, wu_ref[:, c * ck:(c + 1) * ck])
        a = (g * jax.nn.sigmoid(g) * u).astype(BF16)
        part = _dot(a, wo_ref[c * ck:(c + 1) * ck, :])
        acc = part if acc is None else acc + part
    return x + 0.5 * _rms(acc, post_ref[...])


def _ffn_kernel(x_ref, pre_ref, wg_ref, wu_ref, wo_ref, post_ref, o_ref, *, n_chunk):
    o_ref[...] = _ffn_rows(x_ref[...], pre_ref, wg_ref, wu_ref, wo_ref, post_ref, n_chunk)


def _ffn_chunks(f):
    return 2 if f % 256 == 0 else 1


def _ffn(x, pre, wg, wu, wo, post, *, tm):
    t, d = x.shape
    f = wg.shape[1]
    n_chunk = _ffn_chunks(f)
    return pl.pallas_call(
        functools.partial(_ffn_kernel, n_chunk=n_chunk),
        grid=(t // tm,),
        in_specs=[
            pl.BlockSpec((tm, d), lambda i: (i, 0)),
            _const_spec((1, d)), _const_spec((d, f)), _const_spec((d, f)), _const_spec((f, d)),
            _const_spec((1, d)),
        ],
        out_specs=pl.BlockSpec((tm, d), lambda i: (i, 0)),
        out_shape=jax.ShapeDtypeStruct((t, d), F32),
        compiler_params=_cparams(("parallel",)),
        name="ffn",
    )(x, pre, wg, wu, wo, post)


def _rope_heads(x, cos, sin_lo, sin_hi):
    outs = []
    for h in range(x.shape[1] // HEAD_W):
        xh = x[:, h * HEAD_W:(h + 1) * HEAD_W]
        up = pltpu.roll(xh, HEAD_W - A_HD // 2, axis=1)
        dn = pltpu.roll(xh, A_HD // 2, axis=1)
        outs.append(xh * cos + up * sin_lo + dn * sin_hi)
    return outs


def _proj_kernel(h_ref, pre_ref, wq_ref, wk_ref, wv_ref, wc_ref, wz_ref, wab_ref, wga_ref, wgb_ref,
                 cos_ref, slo_ref, shi_ref, alog_ref, dt_ref,
                 q_ref, kf_ref, kb_ref, vf_ref, vb_ref, c_ref, z_ref, gb_ref, ga_ref, gbt_ref, *, heads_t):
    u = _rms(h_ref[...], pre_ref[...]).astype(BF16)
    cos, slo, shi = cos_ref[...], slo_ref[...], shi_ref[...]
    q_scale = (A_HD ** -0.5) * LOG2E
    for h, qh in enumerate(_rope_heads(_dot(u, wq_ref[...]), cos, slo, shi)):
        if heads_t:
            q_ref[0, h] = (qh * q_scale).T.astype(BF16)
        else:
            q_ref[:, h * HEAD_W:(h + 1) * HEAD_W] = (qh * q_scale).astype(BF16)
    for h, kh in enumerate(_rope_heads(_dot(u, wk_ref[...]), cos, slo, shi)):
        kf_ref[:, h * HEAD_W:(h + 1) * HEAD_W] = kh
        kb_ref[:, h * HEAD_W:(h + 1) * HEAD_W] = kh.astype(BF16)
    v = _dot(u, wv_ref[...])
    vf_ref[...] = v
    if heads_t:
        pad_rows = VT_ROWS - HEAD_W
        ones_row = jnp.where(lax.broadcasted_iota(jnp.int32, (pad_rows, v.shape[0]), 0) == 0, 1.0, 0.0)
        for h in range(v.shape[1] // HEAD_W):
            vb_ref[0, h, 0:HEAD_W, :] = v[:, h * HEAD_W:(h + 1) * HEAD_W].T.astype(BF16)
            vb_ref[0, h, HEAD_W:VT_ROWS, :] = ones_row.astype(BF16)
    else:
        vb_ref[...] = v.astype(BF16)
    n_c = wc_ref.shape[1] // wq_ref.shape[1]
    for j in range(n_c):
        w = wq_ref.shape[1]
        c_ref[:, j * w:(j + 1) * w] = _dot(u, wc_ref[:, j * w:(j + 1) * w])
    z_ref[...] = _dot(u, wz_ref[...]).astype(BF16)
    ga_ref[...] = _dot(u, wga_ref[...]).astype(BF16)
    gbt_ref[...] = _dot(u, wgb_ref[...]).astype(BF16)
    ab = _dot(u, wab_ref[...])
    xs = ab + dt_ref[...]
    softplus = jnp.maximum(xs, 0.0) + jnp.log(1.0 + jnp.exp(-jnp.abs(xs)))
    gval = -jnp.exp(alog_ref[...]) * softplus
    lane = lax.broadcasted_iota(jnp.int32, ab.shape, 1)
    gb_ref[...] = jnp.where(lane < B_HEADS, gval, jnp.where(lane < 2 * B_HEADS, jax.nn.sigmoid(ab), 0.0))


def _proj(h, pre, ws, tabs, alog_row, dt_row, *, tm, heads_t):
    t, d = h.shape
    wq, wk, wv, wc, wz, wab, wga, wgb = ws
    cos, slo, shi = tabs
    nt = cos.shape[0] // tm
    row = lambda w: pl.BlockSpec((tm, w), lambda i: (i, 0))
    tab = pl.BlockSpec((tm, HEAD_W), lambda i: (i % nt, 0))
    sd = lambda w, dt: jax.ShapeDtypeStruct((t, w), dt)
    cw = wc.shape[1]
    if heads_t:
        nh = d // HEAD_W
        head_blk = lambda r: pl.BlockSpec((1, nh, r, tm), lambda i: (i // nt, 0, 0, i % nt))
        head_sd = lambda r: jax.ShapeDtypeStruct((t // cos.shape[0], nh, r, cos.shape[0]), BF16)
        q_spec, q_sd, v_spec, v_sd = head_blk(HEAD_W), head_sd(HEAD_W), head_blk(VT_ROWS), head_sd(VT_ROWS)
    else:
        q_spec, q_sd, v_spec, v_sd = row(d), sd(d, BF16), row(d), sd(d, BF16)
    return pl.pallas_call(
        functools.partial(_proj_kernel, heads_t=heads_t),
        grid=(t // tm,),
        in_specs=[row(d), _const_spec((1, d))] + [_const_spec(w.shape) for w in ws]
                 + [tab, tab, tab, _const_spec((1, HEAD_W)), _const_spec((1, HEAD_W))],
        out_specs=[q_spec, row(d), row(d), row(d), v_spec, row(cw), row(d), row(HEAD_W), row(d), row(d)],
        out_shape=[q_sd, sd(d, F32), sd(d, BF16), sd(d, F32), v_sd, sd(cw, F32), sd(d, BF16),
                   sd(HEAD_W, F32), sd(d, BF16), sd(d, BF16)],
        compiler_params=_cparams(("parallel",)),
        name="proj",
    )(h, pre, *ws, cos, slo, shi, alog_row, dt_row)


def _lambda_full(lam_ref, lam_init):
    l = lam_ref[...]
    s1 = jnp.sum(l[0:1] * l[1:2], axis=-1, keepdims=True)
    s2 = jnp.sum(l[2:3] * l[3:4], axis=-1, keepdims=True)
    return jnp.exp(s1) - jnp.exp(s2) + lam_init


def _attn_kernel(qi_ref, ki_ref, lam_ref, sub_ref, qt_ref, k_ref, vt_ref, o_ref,
                 q1_s, q2_s, m_s, acc_s, *, lam_init):
    p = pl.program_id(2)
    ki = ki_ref[p]
    tq, tk = qt_ref.shape[3], k_ref.shape[1]
    off = ki * tk - qi_ref[p] * tq

    @pl.when(ki == 0)
    def _init():
        qt = qt_ref[0, 0].astype(F32)
        row = lax.broadcasted_iota(jnp.int32, qt.shape, 0)
        q1_s[...] = jnp.where(row < A_HD, qt, 0.0).astype(BF16)
        q2_s[...] = jnp.where(row >= A_HD, qt, 0.0).astype(BF16)
        m_s[...] = jnp.full(m_s.shape, NEG_BIG, F32)
        acc_s[...] = jnp.zeros(acc_s.shape, F32)

    def _step(masked):
        k = k_ref[0]
        vt = vt_ref[0, 0]
        if masked:
            krow = lax.broadcasted_iota(jnp.int32, (tk, tq), 0) + off
            qcol = lax.broadcasted_iota(jnp.int32, (tk, tq), 1)
            keep = krow <= qcol
        scores = [_dot(k, qs[...]) for qs in (q1_s, q2_s)]
        probs = []
        for m, s in enumerate(scores):
            if masked:
                s = jnp.where(keep, s, NEG_BIG)
            m_prev = m_s[m]
            m_new = jnp.maximum(m_prev, jnp.max(s, axis=0, keepdims=True))
            probs.append((jnp.exp2(m_prev - m_new), jnp.exp2((s - m_new).astype(BF16))))
            m_s[m] = m_new
        for m, (alpha, pm) in enumerate(probs):
            acc_s[m] = alpha * acc_s[m] + _dot(vt, pm)

    @pl.when(off < 0)
    def _below_diagonal():
        _step(False)

    @pl.when(off >= 0)
    def _on_diagonal():
        _step(True)

    @pl.when(off == tq - tk)
    def _finish():
        lam = _lambda_full(lam_ref, lam_init)
        n0, n1 = acc_s[0], acc_s[1]
        ot = n0[:HEAD_W] / n0[HEAD_W:HEAD_W + 1] - lam * (n1[:HEAD_W] / n1[HEAD_W:HEAD_W + 1])
        inv = lax.rsqrt(jnp.mean(ot * ot, axis=0, keepdims=True) + NORM_EPS)
        on = ot * inv * (sub_ref[...] * (1.0 - lam_init))
        o_ref[0] = on.T.astype(o_ref.dtype)


def _attn_prompt(qt, k, vt, lamv, subln_col, *, lam_init, tq, tk):
    b, l, d = k.shape
    assert tq % tk == 0 and l % tq == 0
    pairs = [(i, j) for i in range(l // tq) for j in range((i + 1) * (tq // tk))]
    qi_tab = jnp.asarray(np.array([pq for pq, _ in pairs], np.int32))
    ki_tab = jnp.asarray(np.array([pk for _, pk in pairs], np.int32))
    qt_spec = pl.BlockSpec((1, 1, HEAD_W, tq), lambda bi, h, p, qt_, kt_: (bi, h, 0, qt_[p]))
    vt_spec = pl.BlockSpec((1, 1, VT_ROWS, tk), lambda bi, h, p, qt_, kt_: (bi, h, 0, kt_[p]))
    k_spec = pl.BlockSpec((1, tk, HEAD_W), lambda bi, h, p, qt_, kt_: (bi, kt_[p], h))
    o_spec = pl.BlockSpec((1, tq, HEAD_W), lambda bi, h, p, qt_, kt_: (bi, qt_[p], h))
    grid_spec = pltpu.PrefetchScalarGridSpec(
        num_scalar_prefetch=2,
        grid=(b, A_HEADS, len(pairs)),
        in_specs=[
            pl.BlockSpec((4, A_HD), lambda bi, h, p, qt_, kt_: (0, 0)),
            pl.BlockSpec((HEAD_W, 1), lambda bi, h, p, qt_, kt_: (0, 0)),
            qt_spec, k_spec, vt_spec,
        ],
        out_specs=o_spec,
        scratch_shapes=[
            pltpu.VMEM((HEAD_W, tq), BF16), pltpu.VMEM((HEAD_W, tq), BF16),
            pltpu.VMEM((2, 1, tq), F32), pltpu.VMEM((2, VT_ROWS, tq), F32),
        ],
    )
    return pl.pallas_call(
        functools.partial(_attn_kernel, lam_init=lam_init),
        grid_spec=grid_spec,
        out_shape=jax.ShapeDtypeStruct((b, l, d), BF16),
        compiler_params=_cparams(("parallel", "parallel", "arbitrary")),
        name="attn_prompt",
    )(qi_tab, ki_tab, lamv, subln_col, qt, k, vt)


N_QROWS = 8
NEW_TOK_PAD = 16


PAGES_PER_STEP = 16


def _attn_decode_kernel(pt_ref, lam_ref, sub_ref, q_ref, kn_ref, vn_ref, *rest, lam_init, n_new, n_pg):
    kc_refs, vc_refs = rest[:n_pg], rest[n_pg:2 * n_pg]
    o_ref, m_s, l_s, acc_s, bias_s = rest[2 * n_pg:]
    pg = pl.program_id(1)
    qall = q_ref[0]
    nr = qall.shape[0]

    def _head_bias(ncols, causal):
        row = lax.broadcasted_iota(jnp.int32, (nr, ncols), 0)
        col = lax.broadcasted_iota(jnp.int32, (nr, ncols), 1)
        keep = (col % A_HEADS) == (row // N_QROWS)
        if causal:
            keep = jnp.logical_and(keep, (col // A_HEADS) <= (row % n_new))
        return jnp.where(keep, 0.0, NEG_BIG).astype(F32)

    def _update(ks, vs, biases):
        ss = [_dot_nt(qall, k) + bias for k, bias in zip(ks, biases)]
        m_prev = m_s[...]
        m_new = m_prev
        for s in ss:
            m_new = jnp.maximum(m_new, jnp.max(s, axis=-1, keepdims=True))
        alpha = jnp.exp2(m_prev - m_new)
        l = alpha * l_s[...]
        acc = alpha * acc_s[...]
        for s, v in zip(ss, vs):
            pm = jnp.exp2(s - m_new)
            l = l + jnp.sum(pm, axis=-1, keepdims=True)
            acc = acc + _dot(pm.astype(BF16), v)
        l_s[...] = l
        acc_s[...] = acc
        m_s[...] = m_new

    rows = kc_refs[0].shape[1] * kc_refs[0].shape[2]

    @pl.when(pg == 0)
    def _new_rows():
        m_s[...] = jnp.full(m_s.shape, NEG_BIG, F32)
        l_s[...] = jnp.zeros(l_s.shape, F32)
        acc_s[...] = jnp.zeros(acc_s.shape, F32)
        bias_s[...] = _head_bias(rows, False)
        kn = kn_ref[0]
        _update([kn], [vn_ref[0]], [_head_bias(kn.shape[0], True)])

    flat = lambda r: r[0].reshape(rows, HEAD_W).astype(BF16)
    bias = bias_s[...]
    _update([flat(r) for r in kc_refs], [flat(r) for r in vc_refs], [bias] * n_pg)

    @pl.when(pg == pl.num_programs(1) - 1)
    def _fin():
        lam = _lambda_full(lam_ref, lam_init)
        nrm = acc_s[...] / l_s[...]
        o = nrm - lam * pltpu.roll(nrm, nr - N_QROWS // 2, axis=0)
        o_ref[0] = (_rms(o, sub_ref[...]) * (1.0 - lam_init)).astype(o_ref.dtype)


def _attn_decode(qall, kn, vn, cache_k, cache_v, page_table, lamv, subln, *, lam_init, n_new):
    bs, n_pages = page_table.shape
    page = cache_k.shape[1]
    nr = qall.shape[1]
    n_pg = math.gcd(PAGES_PER_STEP, n_pages)
    seq = lambda r: pl.BlockSpec((1, r, HEAD_W), lambda b, p, pt: (b, 0, 0))
    cache = [pl.BlockSpec((1, page, A_HEADS, HEAD_W), lambda b, p, pt, j=j: (pt[b, p * n_pg + j], 0, 0, 0))
             for j in range(n_pg)]
    grid_spec = pltpu.PrefetchScalarGridSpec(
        num_scalar_prefetch=1,
        grid=(bs, n_pages // n_pg),
        in_specs=[
            pl.BlockSpec((4, A_HD), lambda b, p, pt: (0, 0)),
            pl.BlockSpec((1, HEAD_W), lambda b, p, pt: (0, 0)),
            seq(nr), seq(kn.shape[1]), seq(vn.shape[1]),
        ] + cache + cache,
        out_specs=seq(nr),
        scratch_shapes=[pltpu.VMEM((nr, 1), F32), pltpu.VMEM((nr, 1), F32), pltpu.VMEM((nr, HEAD_W), F32),
                        pltpu.VMEM((nr, page * A_HEADS), F32)],
    )
    return pl.pallas_call(
        functools.partial(_attn_decode_kernel, lam_init=lam_init, n_new=n_new, n_pg=n_pg),
        grid_spec=grid_spec,
        out_shape=jax.ShapeDtypeStruct((bs, nr, HEAD_W), BF16),
        compiler_params=_cparams(("parallel", "arbitrary")),
        name="attn_decode",
    )(page_table, lamv, subln, qall, kn, vn, *([cache_k] * n_pg), *([cache_v] * n_pg))


CONV_PAD = 8
MXU_DEPTH = 256
SAMPLE_CHUNK = 16
DN_CHUNKS_PER_STEP = 4


def _delta_kernel(x_ref, conv0_ref, gb_ref, z_ref, s0_ref, cw_ref, nw_ref, o_ref, sout_ref, xp_s, st_s, *, c):
    j = pl.program_id(1)
    rows_step = x_ref.shape[1]
    nh = st_s.shape[0]
    dk = st_s.shape[1]
    kw = nh * dk

    @pl.when(j == 0)
    def _init():
        xp_s[0:CONV_PAD, :] = conv0_ref[0]
        st_s[...] = s0_ref[0]

    xp_s[CONV_PAD:CONV_PAD + rows_step, :] = x_ref[0]

    def conv(r0, col):
        acc = None
        for i in range(CONV_W):
            rr = r0 + CONV_PAD - (CONV_W - 1) + i
            term = xp_s[rr:rr + c, col:col + dk] * cw_ref[i:i + 1, col:col + dk]
            acc = term if acc is None else acc + term
        return acc * jax.nn.sigmoid(acc)

    def l2n(x):
        return x * lax.rsqrt(jnp.sum(x * x, axis=-1, keepdims=True) + NORM_EPS)

    r1 = lax.broadcasted_iota(jnp.int32, (c, c), 0)
    c1 = lax.broadcasted_iota(jnp.int32, (c, c), 1)
    ltri = jnp.where(r1 >= c1, 1.0, 0.0).astype(F32)

    grp = max(1, min(nh, MXU_DEPTH // c))
    n = grp * c
    shift = int(math.log2(c))
    ri = lax.broadcasted_iota(jnp.int32, (n, n), 0)
    ci = lax.broadcasted_iota(jnp.int32, (n, n), 1)
    same = (ri >> shift) == (ci >> shift)
    incl = jnp.logical_and(same, ri >= ci)
    strict = jnp.logical_and(same, ri > ci)
    diag = ri == ci
    eye = jnp.where(diag, 1.0, 0.0).astype(F32)
    n_double = shift - 1
    stack = lambda parts: jnp.concatenate(parts, axis=0) if len(parts) > 1 else parts[0]

    def chunk_local(r0):
        gb = gb_ref[0, r0:r0 + c, :]
        gc_all = _dot(ltri, gb, HIGHEST)
        groups = []
        for g0 in range(0, nh, grp):
            heads = range(g0, g0 + grp)
            q = stack([l2n(conv(r0, h * dk)) * (dk ** -0.5) for h in heads])
            k = stack([l2n(conv(r0, kw + h * dk)) for h in heads])
            v = stack([conv(r0, 2 * kw + h * dk) for h in heads])
            beta = stack([gb[:, nh + h:nh + h + 1] for h in heads])
            gc = stack([gc_all[:, h:h + 1] for h in heads])
            gc_last = stack([jnp.broadcast_to(gc_all[c - 1:c, h:h + 1], (c, 1)) for h in heads])
            gc_row = jnp.sum(jnp.where(diag, gc, 0.0), axis=0, keepdims=True)
            decay = jnp.exp(jnp.where(incl, gc - gc_row, NEG_BIG))
            kbeta = k * beta
            k_b = k.astype(BF16)
            a = jnp.where(strict, _dot_nt(kbeta.astype(BF16), k_b) * decay, 0.0)
            tinv = eye - a
            apow = a.astype(BF16)
            for _ in range(n_double):
                apow = _dot(apow, apow).astype(BF16)
                tinv = tinv + _dot(tinv.astype(BF16), apow)
            e_gc = jnp.exp(gc)
            uw = _dot(tinv.astype(BF16), jnp.concatenate([v * beta, kbeta * e_gc], axis=1).astype(BF16))
            groups.append(dict(
                heads=heads, u=uw[:, :dk], w=uw[:, dk:].astype(BF16),
                qk=(_dot_nt(q.astype(BF16), k_b) * decay).astype(BF16),
                q_dec=(q * e_gc).astype(BF16), k_dec=(k * jnp.exp(gc_last - gc)).astype(BF16),
                g_tot=[jnp.exp(gc_all[c - 1:c, h:h + 1]) for h in heads]))
        return groups

    def state_update(r0, groups):
        for grp_v in groups:
            heads = grp_v['heads']
            qs, v_new = [], []
            for i, h in enumerate(heads):
                rows = slice(i * c, (i + 1) * c)
                wq = _dot(jnp.concatenate([grp_v['w'][rows], grp_v['q_dec'][rows]], axis=0),
                          st_s[h].astype(BF16))
                v_new.append(grp_v['u'][rows] - wq[:c])
                qs.append(wq[c:])
            v_new_b = stack(v_new).astype(BF16)
            o = stack(qs) + _dot(grp_v['qk'], v_new_b)
            for i, h in enumerate(heads):
                rows = slice(i * c, (i + 1) * c)
                st_s[h] = st_s[h] * grp_v['g_tot'][i] + _dot_tn(grp_v['k_dec'][rows], v_new_b[rows])
                zh = z_ref[0, r0:r0 + c, h * dk:(h + 1) * dk].astype(F32)
                o_ref[0, r0:r0 + c, h * dk:(h + 1) * dk] = (
                    _rms(o[rows], nw_ref[...]) * (zh * jax.nn.sigmoid(zh))).astype(o_ref.dtype)

    starts = range(0, rows_step, c)
    local = [chunk_local(r0) for r0 in starts]
    for r0, groups in zip(starts, local):
        state_update(r0, groups)

    xp_s[0:CONV_PAD, :] = xp_s[rows_step:rows_step + CONV_PAD, :]

    @pl.when(j == pl.num_programs(1) - 1)
    def _fin():
        sout_ref[0] = st_s[...]


def _delta(x, conv0, gb, z, s0, conv_w, norm_w, *, chunk, chunks_per_step):
    b, l, cc = x.shape
    nh, dk, dv = s0.shape[1:]
    rows_step = chunk * chunks_per_step
    assert l % rows_step == 0
    blk = lambda w: pl.BlockSpec((1, rows_step, w), lambda bi, j: (bi, j, 0))
    return pl.pallas_call(
        functools.partial(_delta_kernel, c=chunk),
        grid=(b, l // rows_step),
        in_specs=[
            blk(cc),
            pl.BlockSpec((1, CONV_PAD, cc), lambda bi, j: (bi, 0, 0)),
            blk(gb.shape[2]), blk(z.shape[2]),
            pl.BlockSpec((1, nh, dk, dv), lambda bi, j: (bi, 0, 0, 0)),
            pl.BlockSpec((CONV_W, cc), lambda bi, j: (0, 0)),
            pl.BlockSpec((1, dv), lambda bi, j: (0, 0)),
        ],
        out_specs=[blk(nh * dv), pl.BlockSpec((1, nh, dk, dv), lambda bi, j: (bi, 0, 0, 0))],
        out_shape=[jax.ShapeDtypeStruct((b, l, nh * dv), BF16), jax.ShapeDtypeStruct(s0.shape, F32)],
        scratch_shapes=[pltpu.VMEM((rows_step + CONV_PAD, cc), F32), pltpu.VMEM((nh, dk, dv), F32)],
        compiler_params=_cparams(("parallel", "arbitrary")),
        name="delta",
    )(x, conv0, gb, z, s0, conv_w, norm_w)


def _merge_ffn_kernel(h_ref, oa_ref, ob_ref, ga_ref, gb_ref, wa_ref, wb_ref, wo_ref, post_ref,
                      pre2_ref, wg_ref, wu_ref, wo2_ref, post2_ref, o_ref, *, n_chunk):
    a = _dot(oa_ref[...], wa_ref[...])
    b = _dot(ob_ref[...], wb_ref[...])
    m = jax.nn.sigmoid(ga_ref[...].astype(F32)) * a + jax.nn.sigmoid(gb_ref[...].astype(F32)) * b
    r = _dot(m.astype(BF16), wo_ref[...])
    h2 = h_ref[...] + _rms(r, post_ref[...])
    o_ref[...] = _ffn_rows(h2, pre2_ref, wg_ref, wu_ref, wo2_ref, post2_ref, n_chunk)


def _merge_ffn(h, oa, ob, ga, gb, wa, wb, wo, post, pre2, wg, wu, wo2, post2, *, tm):
    t, d = h.shape
    f = wg.shape[1]
    row = pl.BlockSpec((tm, d), lambda i: (i, 0))
    return pl.pallas_call(
        functools.partial(_merge_ffn_kernel, n_chunk=_ffn_chunks(f)),
        grid=(t // tm,),
        in_specs=[row, row, row, row, row, _const_spec(wa.shape), _const_spec(wb.shape), _const_spec(wo.shape),
                  _const_spec((1, d)),
                  _const_spec((1, d)), _const_spec((d, f)), _const_spec((d, f)), _const_spec((f, d)),
                  _const_spec((1, d))],
        out_specs=row,
        out_shape=jax.ShapeDtypeStruct((t, d), F32),
        compiler_params=_cparams(("parallel",)),
        name="merge_ffn",
    )(h, oa, ob, ga, gb, wa, wb, wo, post, pre2, wg, wu, wo2, post2)


def _rope_tables(pos):
    half = A_HD // 2
    inv = jnp.power(ROPE_THETA, -2.0 * jnp.arange(half, dtype=F32) / A_HD)
    ang = pos.astype(F32)[:, None] * inv[None, :]
    cos, sin = jnp.cos(ang), jnp.sin(ang)
    zero = jnp.zeros_like(sin)
    reps = HEAD_W // A_HD
    cos_t = jnp.tile(jnp.concatenate([cos, cos], axis=1), (1, reps))
    sin_lo = jnp.tile(jnp.concatenate([-sin, zero], axis=1), (1, reps))
    sin_hi = jnp.tile(jnp.concatenate([zero, sin], axis=1), (1, reps))
    return cos_t, sin_lo, sin_hi


def _pad_rows(x, rows, front=False):
    pad = rows - x.shape[1]
    cfg = [(0, 0)] * x.ndim
    cfg[1] = (pad, 0) if front else (0, pad)
    return jnp.pad(x, cfg)


def _layer(x, tabs, p, lam_init, *, tm, past):
    b, l, d = x.shape
    t = b * l
    h1 = _ffn(x.reshape(t, d), p['ffn1_pre'], p['ffn1_wg'], p['ffn1_wu'], p['ffn1_wo'], p['ffn1_post'], tm=tm)
    q, kf, kb, vf, vb, c, z, gb, ga, gbt = _proj(h1, p['mix_pre'], p['w_in'], tabs, p['alog'], p['dt'],
                                                 tm=min(tm, PROJ_TM), heads_t=past is None)
    cc = c.shape[1]
    c3 = c.reshape(b, l, cc)
    new_conv = c3[:, l - (CONV_W - 1):]
    if past is None:
        o_a = _attn_prompt(q, kb.reshape(b, l, d), vb, p['lamv'], p['subln'].reshape(-1, 1),
                           lam_init=lam_init, tq=min(ATTN_TQ, l), tk=min(ATTN_TK, l))
        conv0 = jnp.zeros((b, CONV_PAD, cc), F32)
        s0 = jnp.zeros((b, B_HEADS, B_DK, B_DK), F32)
        o_b, new_s = _delta(c3, conv0, gb.reshape(b, l, -1), z.reshape(b, l, d), s0, p['conv_w'], p['dnorm'],
                            chunk=DN_CHUNK, chunks_per_step=math.gcd(DN_CHUNKS_PER_STEP, l // DN_CHUNK))
        o_a = o_a.reshape(t, d)
        o_b = o_b.reshape(t, d)
    else:
        cache_k, cache_v, page_table, conv_state, s_state = past
        assert 2 * l == N_QROWS
        qh = q.reshape(b, l, A_HEADS, HEAD_W).transpose(0, 2, 1, 3)[:, :, None]
        lane_map = (jnp.arange(HEAD_W) // A_HD).reshape(1, 1, 1, 1, HEAD_W)
        qall = jnp.where(lane_map == jnp.arange(2).reshape(1, 1, 2, 1, 1), qh, jnp.zeros((), q.dtype))
        qall = qall.reshape(b, A_HEADS * N_QROWS, HEAD_W)
        new_rows = lambda a: _pad_rows(a.reshape(b, l, A_HEADS, HEAD_W), NEW_TOK_PAD).reshape(b, -1, HEAD_W)
        o_a = _attn_decode(qall, new_rows(kb), new_rows(vb), cache_k, cache_v, page_table, p['lamv'], p['subln'],
                           lam_init=lam_init, n_new=l)
        o_a = o_a.reshape(b, A_HEADS, N_QROWS, HEAD_W)[:, :, :l].transpose(0, 2, 1, 3).reshape(t, d)
        lp = -(-l // SAMPLE_CHUNK) * SAMPLE_CHUNK
        o_b, new_s = _delta(_pad_rows(c3, lp), _pad_rows(conv_state, CONV_PAD, front=True),
                            _pad_rows(gb.reshape(b, l, -1), lp), _pad_rows(z.reshape(b, l, d), lp),
                            s_state, p['conv_w'], p['dnorm'], chunk=SAMPLE_CHUNK, chunks_per_step=1)
        o_b = o_b[:, :l].reshape(t, d)
    y = _merge_ffn(h1, o_a, o_b, ga, gbt, p['w_a'], p['w_b'], p['w_o'], p['mix_post'],
                   p['ffn2_pre'], p['ffn2_wg'], p['ffn2_wu'], p['ffn2_wo'], p['ffn2_post'], tm=tm)
    k_rows = kf.reshape(b, l, A_HEADS, HEAD_W)
    v_rows = vf.reshape(b, l, A_HEADS, HEAD_W)
    return y.reshape(b, l, d), k_rows, v_rows, new_conv, new_s


def kernel(x_prompt, x_sample, cache_k, cache_v, state_conv, state_delta, page_table, ffn1_pre_norm, ffn1_w_in, ffn1_w_out, ffn1_post_norm, mix_pre_norm, w_in, conv_w, lambda_q1, lambda_k1, lambda_q2, lambda_k2, attn_subln, a_log, dt_bias, delta_norm, w_a_out, w_b_out, w_o, mix_post_norm, ffn2_pre_norm, ffn2_w_in, ffn2_w_out, ffn2_post_norm):
    depth = w_in.shape[0]
    d = x_prompt.shape[-1]
    lp, ls = x_prompt.shape[1], x_sample.shape[1]
    n_pages = page_table.shape[1]
    page = cache_k.shape[2]
    past_len = n_pages * page
    tabs_p = _rope_tables(jnp.arange(lp, dtype=jnp.int32))
    tabs_s1 = _rope_tables(past_len + jnp.arange(ls, dtype=jnp.int32))
    tabs_s = tuple(jnp.tile(tb, (x_sample.shape[0], 1)) for tb in tabs_s1)
    kw = B_HEADS * B_DK
    splits = np.cumsum([d, d, d, 3 * kw, kw, B_HEADS, B_HEADS, d, d])[:-1].tolist()
    yp, ys = x_prompt, x_sample
    outs = [[] for _ in range(8)]
    for i in range(depth):
        lam_init = 0.8 - 0.6 * math.exp(-0.3 * i)
        wq, wk, wv, wc, wz, wa_, wb_, wga, wgb = jnp.split(w_in[i], splits, axis=1)
        wab = jnp.pad(jnp.concatenate([wa_, wb_], axis=1), ((0, 0), (0, HEAD_W - 2 * B_HEADS)))
        row = lambda v: v.reshape(1, -1).astype(F32)
        lane_pad = lambda v: jnp.pad(v.reshape(1, -1).astype(F32), ((0, 0), (0, HEAD_W - v.shape[-1])))
        f = ffn1_w_in.shape[2] // 2
        p = {
            'ffn1_pre': row(ffn1_pre_norm[i]), 'ffn1_post': row(ffn1_post_norm[i]),
            'ffn1_wg': ffn1_w_in[i, :, :f].astype(BF16), 'ffn1_wu': ffn1_w_in[i, :, f:].astype(BF16),
            'ffn1_wo': ffn1_w_out[i].astype(BF16),
            'ffn2_pre': row(ffn2_pre_norm[i]), 'ffn2_post': row(ffn2_post_norm[i]),
            'ffn2_wg': ffn2_w_in[i, :, :f].astype(BF16), 'ffn2_wu': ffn2_w_in[i, :, f:].astype(BF16),
            'ffn2_wo': ffn2_w_out[i].astype(BF16),
            'mix_pre': row(mix_pre_norm[i]), 'mix_post': row(mix_post_norm[i]),
            'w_in': tuple(w.astype(BF16) for w in (wq, wk, wv, wc, wz, wab, wga, wgb)),
            'alog': lane_pad(a_log[i]), 'dt': lane_pad(dt_bias[i]),
            'lamv': jnp.stack([lambda_q1[i], lambda_k1[i], lambda_q2[i], lambda_k2[i]]).astype(F32),
            'subln': row(attn_subln[i]), 'dnorm': row(delta_norm[i]),
            'conv_w': conv_w[i].astype(F32),
            'w_a': w_a_out[i].astype(BF16), 'w_b': w_b_out[i].astype(BF16), 'w_o': w_o[i].astype(BF16),
        }
        yp, kp, vp, cp, sp = _layer(yp, tabs_p, p, lam_init, tm=PROMPT_TM, past=None)
        ys, ks_, vs_, cs_, ss_ = _layer(ys, tabs_s, p, lam_init, tm=x_sample.shape[0] * ls,
                                        past=(cache_k[i], cache_v[i], page_table, state_conv[i], state_delta[i]))
        for lst, val in zip(outs, (kp, vp, cp, sp, ks_, vs_, cs_, ss_)):
            lst.append(val)
    return (yp, ys) + tuple(jnp.stack(o) for o in outs)
```

```python
import functools
import math

import numpy as np
import jax
import jax.numpy as jnp
from jax import lax
from jax.experimental import pallas as pl
from jax.experimental.pallas import tpu as pltpu

F32 = jnp.float32
BF16 = jnp.bfloat16

NORM_EPS = 1e-6
ROPE_THETA = 10000.0
A_HEADS = 8
A_HD = 64
HEAD_W = 2 * A_HD
B_HEADS = 8
B_DK = 128
CONV_W = 4
DN_CHUNK = 64
NEG_BIG = -1e30
ATTN_TQ = 1024
ATTN_TK = 1024
VT_ROWS = 2 * A_HD + 16
LOG2E = math.log2(math.e)
VMEM_LIMIT_BYTES = 56 * 1024 * 1024
HIGHEST = lax.Precision.HIGHEST
PROMPT_TM = 512
PROJ_TM = 256


def _cparams(sem):
    return pltpu.CompilerParams(dimension_semantics=sem, vmem_limit_bytes=VMEM_LIMIT_BYTES)


def _const_spec(shape):
    zeros = (0,) * len(shape)
    return pl.BlockSpec(shape, lambda *_: zeros, pipeline_mode=pl.Buffered(1))


def _rms(x, w):
    return x * lax.rsqrt(jnp.mean(x * x, axis=-1, keepdims=True) + NORM_EPS) * w


def _silu(x):
    return x * jax.nn.sigmoid(x)


def _dot(a, b, precision=None):
    return jnp.dot(a, b, preferred_element_type=F32, precision=precision)


def _dot_nt(a, b, precision=None):
    return lax.dot_general(a, b, (((1,), (1,)), ((), ())), preferred_element_type=F32, precision=precision)


def _dot_tn(a, b, precision=None):
    return lax.dot_general(a, b, (((0,), (0,)), ((), ())), preferred_element_type=F32, precision=precision)


def _ffn_rows(x, pre_ref, wg_ref, wu_ref, wo_ref, post_ref, n_chunk):
    xn = _rms(x, pre_ref[...]).astype(BF16)
    ck = wg_ref.shape[1] // n_chunk
    acc = None
    for c in range(n_chunk):
        g = _dot(xn, wg_ref[:, c * ck:(c + 1) * ck])
        u = _dot(xn, wu_ref[:, c * ck:(c + 1) * ck])
        a = (g * jax.nn.sigmoid(g) * u).astype(BF16)
        part = _dot(a, wo_ref[c * ck:(c + 1) * ck, :])
        acc = part if acc is None else acc + part
    return x + 0.5 * _rms(acc, post_ref[...])


def _ffn_kernel(x_ref, pre_ref, wg_ref, wu_ref, wo_ref, post_ref, o_ref, *, n_chunk):
    o_ref[...] = _ffn_rows(x_ref[...], pre_ref, wg_ref, wu_ref, wo_ref, post_ref, n_chunk)


def _ffn_chunks(f):
    return 2 if f % 256 == 0 else 1


def _ffn(x, pre, wg, wu, wo, post, *, tm):
    t, d = x.shape
    f = wg.shape[1]
    n_chunk = _ffn_chunks(f)
    return pl.pallas_call(
        functools.partial(_ffn_kernel, n_chunk=n_chunk),
        grid=(t // tm,),
        in_specs=[
            pl.BlockSpec((tm, d), lambda i: (i, 0)),
            _const_spec((1, d)), _const_spec((d, f)), _const_spec((d, f)), _const_spec((f, d)),
            _const_spec((1, d)),
        ],
        out_specs=pl.BlockSpec((tm, d), lambda i: (i, 0)),
        out_shape=jax.ShapeDtypeStruct((t, d), F32),
        compiler_params=_cparams(("parallel",)),
        name="ffn",
    )(x, pre, wg, wu, wo, post)


def _rope_heads(x, cos, sin_lo, sin_hi):
    outs = []
    for h in range(x.shape[1] // HEAD_W):
        xh = x[:, h * HEAD_W:(h + 1) * HEAD_W]
        up = pltpu.roll(xh, HEAD_W - A_HD // 2, axis=1)
        dn = pltpu.roll(xh, A_HD // 2, axis=1)
        outs.append(xh * cos + up * sin_lo + dn * sin_hi)
    return outs


def _proj_kernel(h_ref, pre_ref, wq_ref, wk_ref, wv_ref, wc_ref, wz_ref, wab_ref, wga_ref, wgb_ref,
                 cos_ref, slo_ref, shi_ref, alog_ref, dt_ref,
                 q_ref, kf_ref, kb_ref, vf_ref, vb_ref, c_ref, z_ref, gb_ref, ga_ref, gbt_ref, *, heads_t):
    u = _rms(h_ref[...], pre_ref[...]).astype(BF16)
    cos, slo, shi = cos_ref[...], slo_ref[...], shi_ref[...]
    q_scale = (A_HD ** -0.5) * LOG2E
    for h, qh in enumerate(_rope_heads(_dot(u, wq_ref[...]), cos, slo, shi)):
        if heads_t:
            q_ref[0, h] = (qh * q_scale).T.astype(BF16)
        else:
            q_ref[:, h * HEAD_W:(h + 1) * HEAD_W] = (qh * q_scale).astype(BF16)
    for h, kh in enumerate(_rope_heads(_dot(u, wk_ref[...]), cos, slo, shi)):
        kf_ref[:, h * HEAD_W:(h + 1) * HEAD_W] = kh
        kb_ref[:, h * HEAD_W:(h + 1) * HEAD_W] = kh.astype(BF16)
    v = _dot(u, wv_ref[...])
    vf_ref[...] = v
    if heads_t:
        pad_rows = VT_ROWS - HEAD_W
        ones_row = jnp.where(lax.broadcasted_iota(jnp.int32, (pad_rows, v.shape[0]), 0) == 0, 1.0, 0.0)
        for h in range(v.shape[1] // HEAD_W):
            vb_ref[0, h, 0:HEAD_W, :] = v[:, h * HEAD_W:(h + 1) * HEAD_W].T.astype(BF16)
            vb_ref[0, h, HEAD_W:VT_ROWS, :] = ones_row.astype(BF16)
    else:
        vb_ref[...] = v.astype(BF16)
    n_c = wc_ref.shape[1] // wq_ref.shape[1]
    for j in range(n_c):
        w = wq_ref.shape[1]
        c_ref[:, j * w:(j + 1) * w] = _dot(u, wc_ref[:, j * w:(j + 1) * w])
    z_ref[...] = _dot(u, wz_ref[...]).astype(BF16)
    ga_ref[...] = _dot(u, wga_ref[...]).astype(BF16)
    gbt_ref[...] = _dot(u, wgb_ref[...]).astype(BF16)
    ab = _dot(u, wab_ref[...])
    xs = ab + dt_ref[...]
    softplus = jnp.maximum(xs, 0.0) + jnp.log(1.0 + jnp.exp(-jnp.abs(xs)))
    gval = -jnp.exp(alog_ref[...]) * softplus
    lane = lax.broadcasted_iota(jnp.int32, ab.shape, 1)
    gb_ref[...] = jnp.where(lane < B_HEADS, gval, jnp.where(lane < 2 * B_HEADS, jax.nn.sigmoid(ab), 0.0))


def _proj(h, pre, ws, tabs, alog_row, dt_row, *, tm, heads_t):
    t, d = h.shape
    wq, wk, wv, wc, wz, wab, wga, wgb = ws
    cos, slo, shi = tabs
    nt = cos.shape[0] // tm
    row = lambda w: pl.BlockSpec((tm, w), lambda i: (i, 0))
    tab = pl.BlockSpec((tm, HEAD_W), lambda i: (i % nt, 0))
    sd = lambda w, dt: jax.ShapeDtypeStruct((t, w), dt)
    cw = wc.shape[1]
    if heads_t:
        nh = d // HEAD_W
        head_blk = lambda r: pl.BlockSpec((1, nh, r, tm), lambda i: (i // nt, 0, 0, i % nt))
        head_sd = lambda r: jax.ShapeDtypeStruct((t // cos.shape[0], nh, r, cos.shape[0]), BF16)
        q_spec, q_sd, v_spec, v_sd = head_blk(HEAD_W), head_sd(HEAD_W), head_blk(VT_ROWS), head_sd(VT_ROWS)
    else:
        q_spec, q_sd, v_spec, v_sd = row(d), sd(d, BF16), row(d), sd(d, BF16)
    return pl.pallas_call(
        functools.partial(_proj_kernel, heads_t=heads_t),
        grid=(t // tm,),
        in_specs=[row(d), _const_spec((1, d))] + [_const_spec(w.shape) for w in ws]
                 + [tab, tab, tab, _const_spec((1, HEAD_W)), _const_spec((1, HEAD_W))],
        out_specs=[q_spec, row(d), row(d), row(d), v_spec, row(cw), row(d), row(HEAD_W), row(d), row(d)],
        out_shape=[q_sd, sd(d, F32), sd(d, BF16), sd(d, F32), v_sd, sd(cw, F32), sd(d, BF16),
                   sd(HEAD_W, F32), sd(d, BF16), sd(d, BF16)],
        compiler_params=_cparams(("parallel",)),
        name="proj",
    )(h, pre, *ws, cos, slo, shi, alog_row, dt_row)


def _lambda_full(lam_ref, lam_init):
    l = lam_ref[...]
    s1 = jnp.sum(l[0:1] * l[1:2], axis=-1, keepdims=True)
    s2 = jnp.sum(l[2:3] * l[3:4], axis=-1, keepdims=True)
    return jnp.exp(s1) - jnp.exp(s2) + lam_init


def _attn_kernel(qi_ref, ki_ref, lam_ref, sub_ref, qt_ref, k_ref, vt_ref, o_ref,
                 q1_s, q2_s, m_s, acc_s, *, lam_init):
    p = pl.program_id(2)
    ki = ki_ref[p]
    tq, tk = qt_ref.shape[3], k_ref.shape[1]
    off = ki * tk - qi_ref[p] * tq

    @pl.when(ki == 0)
    def _init():
        qt = qt_ref[0, 0].astype(F32)
        row = lax.broadcasted_iota(jnp.int32, qt.shape, 0)
        q1_s[...] = jnp.where(row < A_HD, qt, 0.0).astype(BF16)
        q2_s[...] = jnp.where(row >= A_HD, qt, 0.0).astype(BF16)
        m_s[...] = jnp.full(m_s.shape, NEG_BIG, F32)
        acc_s[...] = jnp.zeros(acc_s.shape, F32)

    k = k_ref[0]
    vt = vt_ref[0, 0]
    krow = lax.broadcasted_iota(jnp.int32, (tk, tq), 0) + off
    qcol = lax.broadcasted_iota(jnp.int32, (tk, tq), 1)
    keep = krow <= qcol
    scores = [_dot(k, qs[...]) for qs in (q1_s, q2_s)]
    probs = []
    for m, s in enumerate(scores):
        s = jnp.where(keep, s, NEG_BIG).astype(BF16)
        m_prev = m_s[m]
        m_new = jnp.maximum(m_prev, jnp.max(s, axis=0, keepdims=True).astype(F32))
        probs.append((jnp.exp2(m_prev - m_new), jnp.exp2(s - m_new.astype(BF16))))
        m_s[m] = m_new
    for m, (alpha, pm) in enumerate(probs):
        acc_s[m] = alpha * acc_s[m] + _dot(vt, pm)

    @pl.when(off == tq - tk)
    def _finish():
        lam = _lambda_full(lam_ref, lam_init)
        n0, n1 = acc_s[0], acc_s[1]
        ot = n0[:HEAD_W] / n0[HEAD_W:HEAD_W + 1] - lam * (n1[:HEAD_W] / n1[HEAD_W:HEAD_W + 1])
        inv = lax.rsqrt(jnp.mean(ot * ot, axis=0, keepdims=True) + NORM_EPS)
        on = ot * inv * (sub_ref[...] * (1.0 - lam_init))
        o_ref[0] = on.T.astype(o_ref.dtype)


def _attn_prompt(qt, k, vt, lamv, subln_col, *, lam_init, tq, tk):
    b, l, d = k.shape
    assert tq % tk == 0 and l % tq == 0
    pairs = [(i, j) for i in range(l // tq) for j in range((i + 1) * (tq // tk))]
    qi_tab = jnp.asarray(np.array([pq for pq, _ in pairs], np.int32))
    ki_tab = jnp.asarray(np.array([pk for _, pk in pairs], np.int32))
    qt_spec = pl.BlockSpec((1, 1, HEAD_W, tq), lambda bi, h, p, qt_, kt_: (bi, h, 0, qt_[p]))
    vt_spec = pl.BlockSpec((1, 1, VT_ROWS, tk), lambda bi, h, p, qt_, kt_: (bi, h, 0, kt_[p]))
    k_spec = pl.BlockSpec((1, tk, HEAD_W), lambda bi, h, p, qt_, kt_: (bi, kt_[p], h))
    o_spec = pl.BlockSpec((1, tq, HEAD_W), lambda bi, h, p, qt_, kt_: (bi, qt_[p], h))
    grid_spec = pltpu.PrefetchScalarGridSpec(
        num_scalar_prefetch=2,
        grid=(b, A_HEADS, len(pairs)),
        in_specs=[
            pl.BlockSpec((4, A_HD), lambda bi, h, p, qt_, kt_: (0, 0)),
            pl.BlockSpec((HEAD_W, 1), lambda bi, h, p, qt_, kt_: (0, 0)),
            qt_spec, k_spec, vt_spec,
        ],
        out_specs=o_spec,
        scratch_shapes=[
            pltpu.VMEM((HEAD_W, tq), BF16), pltpu.VMEM((HEAD_W, tq), BF16),
            pltpu.VMEM((2, 1, tq), F32), pltpu.VMEM((2, VT_ROWS, tq), F32),
        ],
    )
    return pl.pallas_call(
        functools.partial(_attn_kernel, lam_init=lam_init),
        grid_spec=grid_spec,
        out_shape=jax.ShapeDtypeStruct((b, l, d), BF16),
        compiler_params=_cparams(("parallel", "parallel", "arbitrary")),
        name="attn_prompt",
    )(qi_tab, ki_tab, lamv, subln_col, qt, k, vt)


N_QROWS = 8
NEW_TOK_PAD = 16


PAGES_PER_STEP = 16


def _attn_decode_kernel(pt_ref, lam_ref, sub_ref, q_ref, kn_ref, vn_ref, *rest, lam_init, n_new, n_pg):
    kc_refs, vc_refs = rest[:n_pg], rest[n_pg:2 * n_pg]
    o_ref, m_s, l_s, acc_s, bias_s = rest[2 * n_pg:]
    pg = pl.program_id(1)
    qall = q_ref[0]
    nr = qall.shape[0]

    def _head_bias(ncols, causal):
        row = lax.broadcasted_iota(jnp.int32, (nr, ncols), 0)
        col = lax.broadcasted_iota(jnp.int32, (nr, ncols), 1)
        keep = (col % A_HEADS) == (row // N_QROWS)
        if causal:
            keep = jnp.logical_and(keep, (col // A_HEADS) <= (row % n_new))
        return jnp.where(keep, 0.0, NEG_BIG).astype(F32)

    def _update(ks, vs, biases):
        ss = [_dot_nt(qall, k) + bias for k, bias in zip(ks, biases)]
        m_prev = m_s[...]
        m_new = m_prev
        for s in ss:
            m_new = jnp.maximum(m_new, jnp.max(s, axis=-1, keepdims=True))
        alpha = jnp.exp2(m_prev - m_new)
        l = alpha * l_s[...]
        acc = alpha * acc_s[...]
        for s, v in zip(ss, vs):
            pm = jnp.exp2(s - m_new)
            l = l + jnp.sum(pm, axis=-1, keepdims=True)
            acc = acc + _dot(pm.astype(BF16), v)
        l_s[...] = l
        acc_s[...] = acc
        m_s[...] = m_new

    rows = kc_refs[0].shape[1] * kc_refs[0].shape[2]

    @pl.when(pg == 0)
    def _new_rows():
        m_s[...] = jnp.full(m_s.shape, NEG_BIG, F32)
        l_s[...] = jnp.zeros(l_s.shape, F32)
        acc_s[...] = jnp.zeros(acc_s.shape, F32)
        bias_s[...] = _head_bias(rows, False)
        kn = kn_ref[0]
        _update([kn], [vn_ref[0]], [_head_bias(kn.shape[0], True)])

    flat = lambda r: r[0].reshape(rows, HEAD_W).astype(BF16)
    bias = bias_s[...]
    _update([flat(r) for r in kc_refs], [flat(r) for r in vc_refs], [bias] * n_pg)

    @pl.when(pg == pl.num_programs(1) - 1)
    def _fin():
        lam = _lambda_full(lam_ref, lam_init)
        nrm = acc_s[...] / l_s[...]
        o = nrm - lam * pltpu.roll(nrm, nr - N_QROWS // 2, axis=0)
        o_ref[0] = (_rms(o, sub_ref[...]) * (1.0 - lam_init)).astype(o_ref.dtype)


def _attn_decode(qall, kn, vn, cache_k, cache_v, page_table, lamv, subln, *, lam_init, n_new):
    bs, n_pages = page_table.shape
    page = cache_k.shape[1]
    nr = qall.shape[1]
    n_pg = math.gcd(PAGES_PER_STEP, n_pages)
    seq = lambda r: pl.BlockSpec((1, r, HEAD_W), lambda b, p, pt: (b, 0, 0))
    cache = [pl.BlockSpec((1, page, A_HEADS, HEAD_W), lambda b, p, pt, j=j: (pt[b, p * n_pg + j], 0, 0, 0))
             for j in range(n_pg)]
    grid_spec = pltpu.PrefetchScalarGridSpec(
        num_scalar_prefetch=1,
        grid=(bs, n_pages // n_pg),
        in_specs=[
            pl.BlockSpec((4, A_HD), lambda b, p, pt: (0, 0)),
            pl.BlockSpec((1, HEAD_W), lambda b, p, pt: (0, 0)),
            seq(nr), seq(kn.shape[1]), seq(vn.shape[1]),
        ] + cache + cache,
        out_specs=seq(nr),
        scratch_shapes=[pltpu.VMEM((nr, 1), F32), pltpu.VMEM((nr, 1), F32), pltpu.VMEM((nr, HEAD_W), F32),
                        pltpu.VMEM((nr, page * A_HEADS), F32)],
    )
    return pl.pallas_call(
        functools.partial(_attn_decode_kernel, lam_init=lam_init, n_new=n_new, n_pg=n_pg),
        grid_spec=grid_spec,
        out_shape=jax.ShapeDtypeStruct((bs, nr, HEAD_W), BF16),
        compiler_params=_cparams(("parallel", "arbitrary")),
        name="attn_decode",
    )(page_table, lamv, subln, qall, kn, vn, *([cache_k] * n_pg), *([cache_v] * n_pg))


CONV_PAD = 8
MXU_DEPTH = 256
SAMPLE_CHUNK = 16
DN_CHUNKS_PER_STEP = 4


def _delta_kernel(x_ref, conv0_ref, gb_ref, z_ref, s0_ref, cw_ref, nw_ref, o_ref, sout_ref, xp_s, st_s, *, c):
    j = pl.program_id(1)
    rows_step = x_ref.shape[1]
    nh = st_s.shape[0]
    dk = st_s.shape[1]
    kw = nh * dk

    @pl.when(j == 0)
    def _init():
        xp_s[0:CONV_PAD, :] = conv0_ref[0]
        st_s[...] = s0_ref[0]

    xp_s[CONV_PAD:CONV_PAD + rows_step, :] = x_ref[0]

    def conv(r0, col):
        acc = None
        for i in range(CONV_W):
            rr = r0 + CONV_PAD - (CONV_W - 1) + i
            term = xp_s[rr:rr + c, col:col + dk] * cw_ref[i:i + 1, col:col + dk]
            acc = term if acc is None else acc + term
        return _silu(acc)

    def l2n(x):
        return x * lax.rsqrt(jnp.sum(x * x, axis=-1, keepdims=True) + NORM_EPS)

    r1 = lax.broadcasted_iota(jnp.int32, (c, c), 0)
    c1 = lax.broadcasted_iota(jnp.int32, (c, c), 1)
    ltri = jnp.where(r1 >= c1, 1.0, 0.0).astype(F32)

    grp = max(1, min(nh, MXU_DEPTH // c))
    n = grp * c
    shift = int(math.log2(c))
    ri = lax.broadcasted_iota(jnp.int32, (n, n), 0)
    ci = lax.broadcasted_iota(jnp.int32, (n, n), 1)
    same = (ri >> shift) == (ci >> shift)
    incl = jnp.logical_and(same, ri >= ci)
    strict = jnp.logical_and(same, ri > ci)
    diag = ri == ci
    eye = jnp.where(diag, 1.0, 0.0).astype(F32)
    n_double = shift - 1
    stack = lambda parts: jnp.concatenate(parts, axis=0) if len(parts) > 1 else parts[0]

    starts = range(0, rows_step, c)
    chains = []
    for r0 in starts:
        gb = gb_ref[0, r0:r0 + c, :]
        gc_all = _dot(ltri, gb, HIGHEST)
        for g0 in range(0, nh, grp):
            heads = range(g0, g0 + grp)
            q = stack([l2n(conv(r0, h * dk)) * (dk ** -0.5) for h in heads])
            k = stack([l2n(conv(r0, kw + h * dk)) for h in heads])
            v = stack([conv(r0, 2 * kw + h * dk) for h in heads])
            beta = stack([gb[:, nh + h:nh + h + 1] for h in heads])
            gc = stack([gc_all[:, h:h + 1] for h in heads])
            gc_last = stack([jnp.broadcast_to(gc_all[c - 1:c, h:h + 1], (c, 1)) for h in heads])
            gc_row = jnp.sum(jnp.where(diag, gc, 0.0), axis=0, keepdims=True)
            decay = jnp.exp(jnp.where(incl, gc - gc_row, NEG_BIG))
            kbeta = k * beta
            e_gc = jnp.exp(gc)
            chains.append(dict(
                r0=r0, heads=heads, decay=decay, q_b=q.astype(BF16), k_b=k.astype(BF16),
                kbeta_b=kbeta.astype(BF16),
                rhs=jnp.concatenate([v * beta, kbeta * e_gc], axis=1).astype(BF16),
                q_dec=(q * e_gc).astype(BF16), k_dec=(k * jnp.exp(gc_last - gc)).astype(BF16),
                g_tot=[jnp.exp(gc_all[c - 1:c, h:h + 1]) for h in heads]))
    a_all = [jnp.where(strict, _dot_nt(ch['kbeta_b'], ch['k_b']) * ch['decay'], 0.0) for ch in chains]
    tinv = [eye - a for a in a_all]
    apow = [a.astype(BF16) for a in a_all]
    for _ in range(n_double):
        apow = [_dot(ap, ap).astype(BF16) for ap in apow]
        tinv = [t + _dot(t.astype(BF16), ap) for t, ap in zip(tinv, apow)]
    for ch, t in zip(chains, tinv):
        uw = _dot(t.astype(BF16), ch['rhs'])
        ch['u'] = uw[:, :dk]
        ch['w'] = uw[:, dk:].astype(BF16)
        ch['qk'] = (_dot_nt(ch['q_b'], ch['k_b']) * ch['decay']).astype(BF16)

    def state_update(r0, groups):
        rows_of = lambda i: slice(i * c, (i + 1) * c)
        st_old = {h: st_s[h] for g in groups for h in g['heads']}
        wq = {h: _dot(jnp.concatenate([g['w'][rows_of(i)], g['q_dec'][rows_of(i)]], axis=0),
                      st_old[h].astype(BF16))
              for g in groups for i, h in enumerate(g['heads'])}
        v_new_b = [stack([g['u'][rows_of(i)] - wq[h][:c] for i, h in enumerate(g['heads'])]).astype(BF16)
                   for g in groups]
        o_all = [stack([wq[h][c:] for h in g['heads']]) + _dot(g['qk'], vb)
                 for g, vb in zip(groups, v_new_b)]
        for g, vb in zip(groups, v_new_b):
            for i, h in enumerate(g['heads']):
                st_s[h] = st_old[h] * g['g_tot'][i] + _dot_tn(g['k_dec'][rows_of(i)], vb[rows_of(i)])
        for g, o in zip(groups, o_all):
            for i, h in enumerate(g['heads']):
                zh = z_ref[0, r0:r0 + c, h * dk:(h + 1) * dk].astype(F32)
                o_ref[0, r0:r0 + c, h * dk:(h + 1) * dk] = (
                    _rms(o[rows_of(i)], nw_ref[...]) * _silu(zh)).astype(o_ref.dtype)

    for r0 in starts:
        state_update(r0, [ch for ch in chains if ch['r0'] == r0])

    xp_s[0:CONV_PAD, :] = xp_s[rows_step:rows_step + CONV_PAD, :]

    @pl.when(j == pl.num_programs(1) - 1)
    def _fin():
        sout_ref[0] = st_s[...]


def _delta(x, conv0, gb, z, s0, conv_w, norm_w, *, chunk, chunks_per_step):
    b, l, cc = x.shape
    nh, dk, dv = s0.shape[1:]
    rows_step = chunk * chunks_per_step
    assert l % rows_step == 0
    blk = lambda w: pl.BlockSpec((1, rows_step, w), lambda bi, j: (bi, j, 0))
    return pl.pallas_call(
        functools.partial(_delta_kernel, c=chunk),
        grid=(b, l // rows_step),
        in_specs=[
            blk(cc),
            pl.BlockSpec((1, CONV_PAD, cc), lambda bi, j: (bi, 0, 0)),
            blk(gb.shape[2]), blk(z.shape[2]),
            pl.BlockSpec((1, nh, dk, dv), lambda bi, j: (bi, 0, 0, 0)),
            pl.BlockSpec((CONV_W, cc), lambda bi, j: (0, 0)),
            pl.BlockSpec((1, dv), lambda bi, j: (0, 0)),
        ],
        out_specs=[blk(nh * dv), pl.BlockSpec((1, nh, dk, dv), lambda bi, j: (bi, 0, 0, 0))],
        out_shape=[jax.ShapeDtypeStruct((b, l, nh * dv), BF16), jax.ShapeDtypeStruct(s0.shape, F32)],
        scratch_shapes=[pltpu.VMEM((rows_step + CONV_PAD, cc), F32), pltpu.VMEM((nh, dk, dv), F32)],
        compiler_params=_cparams(("parallel", "arbitrary")),
        name="delta",
    )(x, conv0, gb, z, s0, conv_w, norm_w)


def _merge_ffn_kernel(h_ref, oa_ref, ob_ref, ga_ref, gb_ref, wa_ref, wb_ref, wo_ref, post_ref,
                      pre2_ref, wg_ref, wu_ref, wo2_ref, post2_ref, o_ref, *, n_chunk):
    a = _dot(oa_ref[...], wa_ref[...])
    b = _dot(ob_ref[...], wb_ref[...])
    m = jax.nn.sigmoid(ga_ref[...].astype(F32)) * a + jax.nn.sigmoid(gb_ref[...].astype(F32)) * b
    r = _dot(m.astype(BF16), wo_ref[...])
    h2 = h_ref[...] + _rms(r, post_ref[...])
    o_ref[...] = _ffn_rows(h2, pre2_ref, wg_ref, wu_ref, wo2_ref, post2_ref, n_chunk)


def _merge_ffn(h, oa, ob, ga, gb, wa, wb, wo, post, pre2, wg, wu, wo2, post2, *, tm):
    t, d = h.shape
    f = wg.shape[1]
    row = pl.BlockSpec((tm, d), lambda i: (i, 0))
    return pl.pallas_call(
        functools.partial(_merge_ffn_kernel, n_chunk=_ffn_chunks(f)),
        grid=(t // tm,),
        in_specs=[row, row, row, row, row, _const_spec(wa.shape), _const_spec(wb.shape), _const_spec(wo.shape),
                  _const_spec((1, d)),
                  _const_spec((1, d)), _const_spec((d, f)), _const_spec((d, f)), _const_spec((f, d)),
                  _const_spec((1, d))],
        out_specs=row,
        out_shape=jax.ShapeDtypeStruct((t, d), F32),
        compiler_params=_cparams(("parallel",)),
        name="merge_ffn",
    )(h, oa, ob, ga, gb, wa, wb, wo, post, pre2, wg, wu, wo2, post2)


def _rope_tables(pos):
    half = A_HD // 2
    inv = jnp.power(ROPE_THETA, -2.0 * jnp.arange(half, dtype=F32) / A_HD)
    ang = pos.astype(F32)[:, None] * inv[None, :]
    cos, sin = jnp.cos(ang), jnp.sin(ang)
    zero = jnp.zeros_like(sin)
    reps = HEAD_W // A_HD
    cos_t = jnp.tile(jnp.concatenate([cos, cos], axis=1), (1, reps))
    sin_lo = jnp.tile(jnp.concatenate([-sin, zero], axis=1), (1, reps))
    sin_hi = jnp.tile(jnp.concatenate([zero, sin], axis=1), (1, reps))
    return cos_t, sin_lo, sin_hi


def _pad_rows(x, rows, front=False):
    pad = rows - x.shape[1]
    cfg = [(0, 0)] * x.ndim
    cfg[1] = (pad, 0) if front else (0, pad)
    return jnp.pad(x, cfg)


def _layer(x, tabs, p, lam_init, *, tm, past):
    b, l, d = x.shape
    t = b * l
    h1 = _ffn(x.reshape(t, d), p['ffn1_pre'], p['ffn1_wg'], p['ffn1_wu'], p['ffn1_wo'], p['ffn1_post'], tm=tm)
    q, kf, kb, vf, vb, c, z, gb, ga, gbt = _proj(h1, p['mix_pre'], p['w_in'], tabs, p['alog'], p['dt'],
                                                 tm=min(tm, PROJ_TM), heads_t=past is None)
    cc = c.shape[1]
    c3 = c.reshape(b, l, cc)
    new_conv = c3[:, l - (CONV_W - 1):]
    if past is None:
        o_a = _attn_prompt(q, kb.reshape(b, l, d), vb, p['lamv'], p['subln'].reshape(-1, 1),
                           lam_init=lam_init, tq=min(ATTN_TQ, l), tk=min(ATTN_TK, l))
        conv0 = jnp.zeros((b, CONV_PAD, cc), F32)
        s0 = jnp.zeros((b, B_HEADS, B_DK, B_DK), F32)
        o_b, new_s = _delta(c3, conv0, gb.reshape(b, l, -1), z.reshape(b, l, d), s0, p['conv_w'], p['dnorm'],
                            chunk=DN_CHUNK, chunks_per_step=math.gcd(DN_CHUNKS_PER_STEP, l // DN_CHUNK))
        o_a = o_a.reshape(t, d)
        o_b = o_b.reshape(t, d)
    else:
        cache_k, cache_v, page_table, conv_state, s_state = past
        assert 2 * l == N_QROWS
        qh = q.reshape(b, l, A_HEADS, HEAD_W).transpose(0, 2, 1, 3)[:, :, None]
        lane_map = (jnp.arange(HEAD_W) // A_HD).reshape(1, 1, 1, 1, HEAD_W)
        qall = jnp.where(lane_map == jnp.arange(2).reshape(1, 1, 2, 1, 1), qh, jnp.zeros((), q.dtype))
        qall = qall.reshape(b, A_HEADS * N_QROWS, HEAD_W)
        new_rows = lambda a: _pad_rows(a.reshape(b, l, A_HEADS, HEAD_W), NEW_TOK_PAD).reshape(b, -1, HEAD_W)
        o_a = _attn_decode(qall, new_rows(kb), new_rows(vb), cache_k, cache_v, page_table, p['lamv'], p['subln'],
                           lam_init=lam_init, n_new=l)
        o_a = o_a.reshape(b, A_HEADS, N_QROWS, HEAD_W)[:, :, :l].transpose(0, 2, 1, 3).reshape(t, d)
        lp = -(-l // SAMPLE_CHUNK) * SAMPLE_CHUNK
        o_b, new_s = _delta(_pad_rows(c3, lp), _pad_rows(conv_state, CONV_PAD, front=True),
                            _pad_rows(gb.reshape(b, l, -1), lp), _pad_rows(z.reshape(b, l, d), lp),
                            s_state, p['conv_w'], p['dnorm'], chunk=SAMPLE_CHUNK, chunks_per_step=1)
        o_b = o_b[:, :l].reshape(t, d)
    y = _merge_ffn(h1, o_a, o_b, ga, gbt, p['w_a'], p['w_b'], p['w_o'], p['mix_post'],
                   p['ffn2_pre'], p['ffn2_wg'], p['ffn2_wu'], p['ffn2_wo'], p['ffn2_post'], tm=tm)
    k_rows = kf.reshape(b, l, A_HEADS, HEAD_W)
    v_rows = vf.reshape(b, l, A_HEADS, HEAD_W)
    return y.reshape(b, l, d), k_rows, v_rows, new_conv, new_s


def kernel(x_prompt, x_sample, cache_k, cache_v, state_conv, state_delta, page_table, ffn1_pre_norm, ffn1_w_in, ffn1_w_out, ffn1_post_norm, mix_pre_norm, w_in, conv_w, lambda_q1, lambda_k1, lambda_q2, lambda_k2, attn_subln, a_log, dt_bias, delta_norm, w_a_out, w_b_out, w_o, mix_post_norm, ffn2_pre_norm, ffn2_w_in, ffn2_w_out, ffn2_post_norm):
    depth = w_in.shape[0]
    d = x_prompt.shape[-1]
    lp, ls = x_prompt.shape[1], x_sample.shape[1]
    n_pages = page_table.shape[1]
    page = cache_k.shape[2]
    past_len = n_pages * page
    tabs_p = _rope_tables(jnp.arange(lp, dtype=jnp.int32))
    tabs_s1 = _rope_tables(past_len + jnp.arange(ls, dtype=jnp.int32))
    tabs_s = tuple(jnp.tile(tb, (x_sample.shape[0], 1)) for tb in tabs_s1)
    kw = B_HEADS * B_DK
    splits = np.cumsum([d, d, d, 3 * kw, kw, B_HEADS, B_HEADS, d, d])[:-1].tolist()
    yp, ys = x_prompt, x_sample
    outs = [[] for _ in range(8)]
    for i in range(depth):
        lam_init = 0.8 - 0.6 * math.exp(-0.3 * i)
        wq, wk, wv, wc, wz, wa_, wb_, wga, wgb = jnp.split(w_in[i], splits, axis=1)
        wab = jnp.pad(jnp.concatenate([wa_, wb_], axis=1), ((0, 0), (0, HEAD_W - 2 * B_HEADS)))
        row = lambda v: v.reshape(1, -1).astype(F32)
        lane_pad = lambda v: jnp.pad(v.reshape(1, -1).astype(F32), ((0, 0), (0, HEAD_W - v.shape[-1])))
        f = ffn1_w_in.shape[2] // 2
        p = {
            'ffn1_pre': row(ffn1_pre_norm[i]), 'ffn1_post': row(ffn1_post_norm[i]),
            'ffn1_wg': ffn1_w_in[i, :, :f].astype(BF16), 'ffn1_wu': ffn1_w_in[i, :, f:].astype(BF16),
            'ffn1_wo': ffn1_w_out[i].astype(BF16),
            'ffn2_pre': row(ffn2_pre_norm[i]), 'ffn2_post': row(ffn2_post_norm[i]),
            'ffn2_wg': ffn2_w_in[i, :, :f].astype(BF16), 'ffn2_wu': ffn2_w_in[i, :, f:].astype(BF16),
            'ffn2_wo': ffn2_w_out[i].astype(BF16),
            'mix_pre': row(mix_pre_norm[i]), 'mix_post': row(mix_post_norm[i]),
            'w_in': tuple(w.astype(BF16) for w in (wq, wk, wv, wc, wz, wab, wga, wgb)),
            'alog': lane_pad(a_log[i]), 'dt': lane_pad(dt_bias[i]),
            'lamv': jnp.stack([lambda_q1[i], lambda_k1[i], lambda_q2[i], lambda_k2[i]]).astype(F32),
            'subln': row(attn_subln[i]), 'dnorm': row(delta_norm[i]),
            'conv_w': conv_w[i].astype(F32),
            'w_a': w_a_out[i].astype(BF16), 'w_b': w_b_out[i].astype(BF16), 'w_o': w_o[i].astype(BF16),
        }
        yp, kp, vp, cp, sp = _layer(yp, tabs_p, p, lam_init, tm=PROMPT_TM, past=None)
        ys, ks_, vs_, cs_, ss_ = _layer(ys, tabs_s, p, lam_init, tm=x_sample.shape[0] * ls,
                                        past=(cache_k[i], cache_v[i], page_table, state_conv[i], state_delta[i]))
        for lst, val in zip(outs, (kp, vp, cp, sp, ks_, vs_, cs_, ss_)):
            lst.append(val)
    return (yp, ys) + tuple(jnp.stack(o) for o in outs)
```

```python
import functools
import math

import numpy as np
import jax
import jax.numpy as jnp
from jax import lax
from jax.experimental import pallas as pl
from jax.experimental.pallas import tpu as pltpu

F32 = jnp.float32
BF16 = jnp.bfloat16

NORM_EPS = 1e-6
ROPE_THETA = 10000.0
A_HEADS = 8
A_HD = 64
HEAD_W = 2 * A_HD
B_HEADS = 8
B_DK = 128
CONV_W = 4
DN_CHUNK = 64
NEG_BIG = -1e30
ATTN_TQ = 1024
ATTN_TK = 1024
ATTN_HEADS_PER_STEP = 2
VT_ROWS = 2 * A_HD + 16
LOG2E = math.log2(math.e)
VMEM_LIMIT_BYTES = 56 * 1024 * 1024
HIGHEST = lax.Precision.HIGHEST
PROMPT_TM = 512
PROJ_TM = 256


def _cparams(sem):
    return pltpu.CompilerParams(dimension_semantics=sem, vmem_limit_bytes=VMEM_LIMIT_BYTES)


def _const_spec(shape):
    zeros = (0,) * len(shape)
    return pl.BlockSpec(shape, lambda *_: zeros, pipeline_mode=pl.Buffered(1))


def _rms(x, w):
    return x * lax.rsqrt(jnp.mean(x * x, axis=-1, keepdims=True) + NORM_EPS) * w


def _silu(x):
    return x * jax.nn.sigmoid(x)


def _dot(a, b, precision=None):
    return jnp.dot(a, b, preferred_element_type=F32, precision=precision)


def _dot_nt(a, b, precision=None):
    return lax.dot_general(a, b, (((1,), (1,)), ((), ())), preferred_element_type=F32, precision=precision)


def _dot_tn(a, b, precision=None):
    return lax.dot_general(a, b, (((0,), (0,)), ((), ())), preferred_element_type=F32, precision=precision)


def _ffn_rows(x, pre_ref, wg_ref, wu_ref, wo_ref, post_ref):
    xn = _rms(x, pre_ref[...]).astype(BF16)
    g = _dot(xn, wg_ref[...])
    u = _dot(xn, wu_ref[...])
    a = (g * jax.nn.sigmoid(g) * u).astype(BF16)
    return x + 0.5 * _rms(_dot(a, wo_ref[...]), post_ref[...])


def _ffn_kernel(x_ref, pre_ref, wg_ref, wu_ref, wo_ref, post_ref, o_ref):
    o_ref[...] = _ffn_rows(x_ref[...], pre_ref, wg_ref, wu_ref, wo_ref, post_ref)


def _ffn(x, pre, wg, wu, wo, post, *, tm):
    t, d = x.shape
    f = wg.shape[1]
    return pl.pallas_call(
        _ffn_kernel,
        grid=(t // tm,),
        in_specs=[
            pl.BlockSpec((tm, d), lambda i: (i, 0)),
            _const_spec((1, d)), _const_spec((d, f)), _const_spec((d, f)), _const_spec((f, d)),
            _const_spec((1, d)),
        ],
        out_specs=pl.BlockSpec((tm, d), lambda i: (i, 0)),
        out_shape=jax.ShapeDtypeStruct((t, d), F32),
        compiler_params=_cparams(("parallel",)),
        name="ffn",
    )(x, pre, wg, wu, wo, post)


def _rope_heads(x, cos, sin_lo, sin_hi):
    outs = []
    for h in range(x.shape[1] // HEAD_W):
        xh = x[:, h * HEAD_W:(h + 1) * HEAD_W]
        up = pltpu.roll(xh, HEAD_W - A_HD // 2, axis=1)
        dn = pltpu.roll(xh, A_HD // 2, axis=1)
        outs.append(xh * cos + up * sin_lo + dn * sin_hi)
    return outs


def _proj_kernel(h_ref, pre_ref, wq_ref, wk_ref, wv_ref, wc_ref, wz_ref, wab_ref, wga_ref, wgb_ref,
                 cos_ref, slo_ref, shi_ref, alog_ref, dt_ref,
                 q_ref, kf_ref, kb_ref, vf_ref, vb_ref, c_ref, z_ref, gb_ref, ga_ref, gbt_ref, *, heads_t):
    u = _rms(h_ref[...], pre_ref[...]).astype(BF16)
    cos, slo, shi = cos_ref[...], slo_ref[...], shi_ref[...]
    q_scale = (A_HD ** -0.5) * LOG2E
    for h, qh in enumerate(_rope_heads(_dot(u, wq_ref[...]), cos, slo, shi)):
        if heads_t:
            q_ref[0, h] = (qh * q_scale).T.astype(BF16)
        else:
            q_ref[:, h * HEAD_W:(h + 1) * HEAD_W] = (qh * q_scale).astype(BF16)
    for h, kh in enumerate(_rope_heads(_dot(u, wk_ref[...]), cos, slo, shi)):
        kf_ref[:, h * HEAD_W:(h + 1) * HEAD_W] = kh
        kb_ref[:, h * HEAD_W:(h + 1) * HEAD_W] = kh.astype(BF16)
    v = _dot(u, wv_ref[...])
    vf_ref[...] = v
    if heads_t:
        pad_rows = VT_ROWS - HEAD_W
        ones_row = jnp.where(lax.broadcasted_iota(jnp.int32, (pad_rows, v.shape[0]), 0) == 0, 1.0, 0.0)
        for h in range(v.shape[1] // HEAD_W):
            vb_ref[0, h, 0:HEAD_W, :] = v[:, h * HEAD_W:(h + 1) * HEAD_W].T.astype(BF16)
            vb_ref[0, h, HEAD_W:VT_ROWS, :] = ones_row.astype(BF16)
    else:
        vb_ref[...] = v.astype(BF16)
    n_c = wc_ref.shape[1] // wq_ref.shape[1]
    for j in range(n_c):
        w = wq_ref.shape[1]
        c_ref[:, j * w:(j + 1) * w] = _dot(u, wc_ref[:, j * w:(j + 1) * w])
    z_ref[...] = _dot(u, wz_ref[...]).astype(BF16)
    ga_ref[...] = _dot(u, wga_ref[...]).astype(BF16)
    gbt_ref[...] = _dot(u, wgb_ref[...]).astype(BF16)
    ab = _dot(u, wab_ref[...])
    xs = ab + dt_ref[...]
    softplus = jnp.maximum(xs, 0.0) + jnp.log(1.0 + jnp.exp(-jnp.abs(xs)))
    gval = -jnp.exp(alog_ref[...]) * softplus
    lane = lax.broadcasted_iota(jnp.int32, ab.shape, 1)
    gb_ref[...] = jnp.where(lane < B_HEADS, gval, jnp.where(lane < 2 * B_HEADS, jax.nn.sigmoid(ab), 0.0))


def _proj(h, pre, ws, tabs, alog_row, dt_row, *, tm, heads_t):
    t, d = h.shape
    wq, wk, wv, wc, wz, wab, wga, wgb = ws
    cos, slo, shi = tabs
    nt = cos.shape[0] // tm
    row = lambda w: pl.BlockSpec((tm, w), lambda i: (i, 0))
    tab = pl.BlockSpec((tm, HEAD_W), lambda i: (i % nt, 0))
    sd = lambda w, dt: jax.ShapeDtypeStruct((t, w), dt)
    cw = wc.shape[1]
    if heads_t:
        nh = d // HEAD_W
        head_blk = lambda r: pl.BlockSpec((1, nh, r, tm), lambda i: (i // nt, 0, 0, i % nt))
        head_sd = lambda r: jax.ShapeDtypeStruct((t // cos.shape[0], nh, r, cos.shape[0]), BF16)
        q_spec, q_sd, v_spec, v_sd = head_blk(HEAD_W), head_sd(HEAD_W), head_blk(VT_ROWS), head_sd(VT_ROWS)
    else:
        q_spec, q_sd, v_spec, v_sd = row(d), sd(d, BF16), row(d), sd(d, BF16)
    return pl.pallas_call(
        functools.partial(_proj_kernel, heads_t=heads_t),
        grid=(t // tm,),
        in_specs=[row(d), _const_spec((1, d))] + [_const_spec(w.shape) for w in ws]
                 + [tab, tab, tab, _const_spec((1, HEAD_W)), _const_spec((1, HEAD_W))],
        out_specs=[q_spec, row(d), row(d), row(d), v_spec, row(cw), row(d), row(HEAD_W), row(d), row(d)],
        out_shape=[q_sd, sd(d, F32), sd(d, BF16), sd(d, F32), v_sd, sd(cw, F32), sd(d, BF16),
                   sd(HEAD_W, F32), sd(d, BF16), sd(d, BF16)],
        compiler_params=_cparams(("parallel",)),
        name="proj",
    )(h, pre, *ws, cos, slo, shi, alog_row, dt_row)


def _lambda_full(lam_ref, lam_init):
    l = lam_ref[...]
    s1 = jnp.sum(l[0:1] * l[1:2], axis=-1, keepdims=True)
    s2 = jnp.sum(l[2:3] * l[3:4], axis=-1, keepdims=True)
    return jnp.exp(s1) - jnp.exp(s2) + lam_init


def _attn_kernel(qi_ref, ki_ref, lam_ref, sub_ref, qt_ref, k_ref, vt_ref, o_ref, q_s, m_s, acc_s, *, lam_init):
    p = pl.program_id(2)
    ki = ki_ref[p]
    nhs, tq, tk = qt_ref.shape[1], qt_ref.shape[3], k_ref.shape[1]
    off = ki * tk - qi_ref[p] * tq

    @pl.when(ki == 0)
    def _init():
        for h in range(nhs):
            qt = qt_ref[0, h].astype(F32)
            row = lax.broadcasted_iota(jnp.int32, qt.shape, 0)
            q_s[2 * h] = jnp.where(row < A_HD, qt, 0.0).astype(BF16)
            q_s[2 * h + 1] = jnp.where(row >= A_HD, qt, 0.0).astype(BF16)
        m_s[...] = jnp.full(m_s.shape, NEG_BIG, F32)
        acc_s[...] = jnp.zeros(acc_s.shape, F32)

    krow = lax.broadcasted_iota(jnp.int32, (tk, tq), 0) + off
    qcol = lax.broadcasted_iota(jnp.int32, (tk, tq), 1)
    keep = krow <= qcol
    chains = range(2 * nhs)
    scores = [_dot(k_ref[0, :, (c // 2) * HEAD_W:(c // 2 + 1) * HEAD_W], q_s[c]) for c in chains]
    for h in range(nhs):
        probs = []
        for c in (2 * h, 2 * h + 1):
            s = jnp.where(keep, scores[c], NEG_BIG).astype(BF16)
            m_prev = m_s[c]
            m_new = jnp.maximum(m_prev, jnp.max(s, axis=0, keepdims=True).astype(F32))
            probs.append((c, jnp.exp2(m_prev - m_new), jnp.exp2(s - m_new.astype(BF16))))
            m_s[c] = m_new
        for c, alpha, pm in probs:
            acc_s[c] = alpha * acc_s[c] + _dot(vt_ref[0, h], pm)

    @pl.when(off == tq - tk)
    def _finish():
        lam = _lambda_full(lam_ref, lam_init)
        for h in range(nhs):
            n0, n1 = acc_s[2 * h], acc_s[2 * h + 1]
            ot = n0[:HEAD_W] / n0[HEAD_W:HEAD_W + 1] - lam * (n1[:HEAD_W] / n1[HEAD_W:HEAD_W + 1])
            inv = lax.rsqrt(jnp.mean(ot * ot, axis=0, keepdims=True) + NORM_EPS)
            on = ot * inv * (sub_ref[...] * (1.0 - lam_init))
            o_ref[0, :, h * HEAD_W:(h + 1) * HEAD_W] = on.T.astype(o_ref.dtype)


def _attn_prompt(qt, k, vt, lamv, subln_col, *, lam_init, tq, tk):
    b, l, d = k.shape
    assert tq % tk == 0 and l % tq == 0
    pairs = [(i, j) for i in range(l // tq) for j in range((i + 1) * (tq // tk))]
    qi_tab = jnp.asarray(np.array([pq for pq, _ in pairs], np.int32))
    ki_tab = jnp.asarray(np.array([pk for _, pk in pairs], np.int32))
    nhs = ATTN_HEADS_PER_STEP
    qt_spec = pl.BlockSpec((1, nhs, HEAD_W, tq), lambda bi, h, p, qt_, kt_: (bi, h, 0, qt_[p]))
    vt_spec = pl.BlockSpec((1, nhs, VT_ROWS, tk), lambda bi, h, p, qt_, kt_: (bi, h, 0, kt_[p]))
    k_spec = pl.BlockSpec((1, tk, nhs * HEAD_W), lambda bi, h, p, qt_, kt_: (bi, kt_[p], h))
    o_spec = pl.BlockSpec((1, tq, nhs * HEAD_W), lambda bi, h, p, qt_, kt_: (bi, qt_[p], h))
    grid_spec = pltpu.PrefetchScalarGridSpec(
        num_scalar_prefetch=2,
        grid=(b, A_HEADS // nhs, len(pairs)),
        in_specs=[
            pl.BlockSpec((4, A_HD), lambda bi, h, p, qt_, kt_: (0, 0)),
            pl.BlockSpec((HEAD_W, 1), lambda bi, h, p, qt_, kt_: (0, 0)),
            qt_spec, k_spec, vt_spec,
        ],
        out_specs=o_spec,
        scratch_shapes=[
            pltpu.VMEM((2 * nhs, HEAD_W, tq), BF16),
            pltpu.VMEM((2 * nhs, 1, tq), F32), pltpu.VMEM((2 * nhs, VT_ROWS, tq), F32),
        ],
    )
    return pl.pallas_call(
        functools.partial(_attn_kernel, lam_init=lam_init),
        grid_spec=grid_spec,
        out_shape=jax.ShapeDtypeStruct((b, l, d), BF16),
        compiler_params=_cparams(("parallel", "parallel", "arbitrary")),
        name="attn_prompt",
    )(qi_tab, ki_tab, lamv, subln_col, qt, k, vt)


N_QROWS = 8
NEW_TOK_PAD = 16


PAGES_PER_STEP = 16


def _attn_decode_kernel(pt_ref, lam_ref, sub_ref, q_ref, kn_ref, vn_ref, *rest, lam_init, n_new, n_pg):
    kc_refs, vc_refs = rest[:n_pg], rest[n_pg:2 * n_pg]
    o_ref, m_s, l_s, acc_s, bias_s = rest[2 * n_pg:]
    pg = pl.program_id(1)
    qall = q_ref[0]
    nr = qall.shape[0]

    def _head_bias(ncols, causal):
        row = lax.broadcasted_iota(jnp.int32, (nr, ncols), 0)
        col = lax.broadcasted_iota(jnp.int32, (nr, ncols), 1)
        keep = (col % A_HEADS) == (row // N_QROWS)
        if causal:
            keep = jnp.logical_and(keep, (col // A_HEADS) <= (row % n_new))
        return jnp.where(keep, 0.0, NEG_BIG).astype(F32)

    def _update(ks, vs, biases):
        ss = [_dot_nt(qall, k) + bias for k, bias in zip(ks, biases)]
        m_prev = m_s[...]
        m_new = m_prev
        for s in ss:
            m_new = jnp.maximum(m_new, jnp.max(s, axis=-1, keepdims=True))
        alpha = jnp.exp2(m_prev - m_new)
        l = alpha * l_s[...]
        acc = alpha * acc_s[...]
        for s, v in zip(ss, vs):
            pm = jnp.exp2(s - m_new)
            l = l + jnp.sum(pm, axis=-1, keepdims=True)
            acc = acc + _dot(pm.astype(BF16), v)
        l_s[...] = l
        acc_s[...] = acc
        m_s[...] = m_new

    rows = kc_refs[0].shape[1] * kc_refs[0].shape[2]

    @pl.when(pg == 0)
    def _new_rows():
        m_s[...] = jnp.full(m_s.shape, NEG_BIG, F32)
        l_s[...] = jnp.zeros(l_s.shape, F32)
        acc_s[...] = jnp.zeros(acc_s.shape, F32)
        bias_s[...] = _head_bias(rows, False)
        kn = kn_ref[0]
        _update([kn], [vn_ref[0]], [_head_bias(kn.shape[0], True)])

    flat = lambda r: r[0].reshape(rows, HEAD_W).astype(BF16)
    bias = bias_s[...]
    _update([flat(r) for r in kc_refs], [flat(r) for r in vc_refs], [bias] * n_pg)

    @pl.when(pg == pl.num_programs(1) - 1)
    def _fin():
        lam = _lambda_full(lam_ref, lam_init)
        nrm = acc_s[...] / l_s[...]
        o = nrm - lam * pltpu.roll(nrm, nr - N_QROWS // 2, axis=0)
        o_ref[0] = (_rms(o, sub_ref[...]) * (1.0 - lam_init)).astype(o_ref.dtype)


def _attn_decode(qall, kn, vn, cache_k, cache_v, page_table, lamv, subln, *, lam_init, n_new):
    bs, n_pages = page_table.shape
    page = cache_k.shape[1]
    nr = qall.shape[1]
    n_pg = math.gcd(PAGES_PER_STEP, n_pages)
    seq = lambda r: pl.BlockSpec((1, r, HEAD_W), lambda b, p, pt: (b, 0, 0))
    cache = [pl.BlockSpec((1, page, A_HEADS, HEAD_W), lambda b, p, pt, j=j: (pt[b, p * n_pg + j], 0, 0, 0))
             for j in range(n_pg)]
    grid_spec = pltpu.PrefetchScalarGridSpec(
        num_scalar_prefetch=1,
        grid=(bs, n_pages // n_pg),
        in_specs=[
            pl.BlockSpec((4, A_HD), lambda b, p, pt: (0, 0)),
            pl.BlockSpec((1, HEAD_W), lambda b, p, pt: (0, 0)),
            seq(nr), seq(kn.shape[1]), seq(vn.shape[1]),
        ] + cache + cache,
        out_specs=seq(nr),
        scratch_shapes=[pltpu.VMEM((nr, 1), F32), pltpu.VMEM((nr, 1), F32), pltpu.VMEM((nr, HEAD_W), F32),
                        pltpu.VMEM((nr, page * A_HEADS), F32)],
    )
    return pl.pallas_call(
        functools.partial(_attn_decode_kernel, lam_init=lam_init, n_new=n_new, n_pg=n_pg),
        grid_spec=grid_spec,
        out_shape=jax.ShapeDtypeStruct((bs, nr, HEAD_W), BF16),
        compiler_params=_cparams(("parallel", "arbitrary")),
        name="attn_decode",
    )(page_table, lamv, subln, qall, kn, vn, *([cache_k] * n_pg), *([cache_v] * n_pg))


CONV_PAD = 8
MXU_DEPTH = 256
SAMPLE_CHUNK = 16
DN_CHUNKS_PER_STEP = 4


def _delta_kernel(x_ref, conv0_ref, gb_ref, z_ref, s0_ref, cw_ref, nw_ref, o_ref, sout_ref, xp_s, st_s, *, c):
    j = pl.program_id(1)
    rows_step = x_ref.shape[1]
    nh = st_s.shape[0]
    dk = st_s.shape[1]
    kw = nh * dk

    @pl.when(j == 0)
    def _init():
        xp_s[0:CONV_PAD, :] = conv0_ref[0]
        st_s[...] = s0_ref[0]

    xp_s[CONV_PAD:CONV_PAD + rows_step, :] = x_ref[0]

    def conv(r0, col):
        acc = None
        for i in range(CONV_W):
            rr = r0 + CONV_PAD - (CONV_W - 1) + i
            term = xp_s[rr:rr + c, col:col + dk] * cw_ref[i:i + 1, col:col + dk]
            acc = term if acc is None else acc + term
        return _silu(acc)

    def l2n(x):
        return x * lax.rsqrt(jnp.sum(x * x, axis=-1, keepdims=True) + NORM_EPS)

    r1 = lax.broadcasted_iota(jnp.int32, (c, c), 0)
    c1 = lax.broadcasted_iota(jnp.int32, (c, c), 1)
    ltri = jnp.where(r1 >= c1, 1.0, 0.0).astype(F32)

    grp = max(1, min(nh, MXU_DEPTH // c))
    n = grp * c
    shift = int(math.log2(c))
    ri = lax.broadcasted_iota(jnp.int32, (n, n), 0)
    ci = lax.broadcasted_iota(jnp.int32, (n, n), 1)
    same = (ri >> shift) == (ci >> shift)
    incl = jnp.logical_and(same, ri >= ci)
    strict = jnp.logical_and(same, ri > ci)
    diag = ri == ci
    eye = jnp.where(diag, 1.0, 0.0).astype(F32)
    n_double = shift - 1
    stack = lambda parts: jnp.concatenate(parts, axis=0) if len(parts) > 1 else parts[0]

    starts = range(0, rows_step, c)
    chains = []
    for r0 in starts:
        gb = gb_ref[0, r0:r0 + c, :]
        gc_all = _dot(ltri, gb, HIGHEST)
        for g0 in range(0, nh, grp):
            heads = range(g0, g0 + grp)
            q = stack([l2n(conv(r0, h * dk)) * (dk ** -0.5) for h in heads])
            k = stack([l2n(conv(r0, kw + h * dk)) for h in heads])
            v = stack([conv(r0, 2 * kw + h * dk) for h in heads])
            beta = stack([gb[:, nh + h:nh + h + 1] for h in heads])
            gc = stack([gc_all[:, h:h + 1] for h in heads])
            gc_last = stack([jnp.broadcast_to(gc_all[c - 1:c, h:h + 1], (c, 1)) for h in heads])
            gc_row = jnp.sum(jnp.where(diag, gc, 0.0), axis=0, keepdims=True)
            decay = jnp.exp(jnp.where(incl, gc - gc_row, NEG_BIG))
            kbeta = k * beta
            e_gc = jnp.exp(gc)
            chains.append(dict(
                r0=r0, heads=heads, decay=decay, q_b=q.astype(BF16), k_b=k.astype(BF16),
                kbeta_b=kbeta.astype(BF16),
                rhs=jnp.concatenate([v * beta, kbeta * e_gc], axis=1).astype(BF16),
                q_dec=(q * e_gc).astype(BF16), k_dec=(k * jnp.exp(gc_last - gc)).astype(BF16),
                g_tot=[jnp.exp(gc_all[c - 1:c, h:h + 1]) for h in heads]))
    a_all = [jnp.where(strict, _dot_nt(ch['kbeta_b'], ch['k_b']) * ch['decay'], 0.0) for ch in chains]
    tinv = [eye - a for a in a_all]
    apow = [a.astype(BF16) for a in a_all]
    for _ in range(n_double):
        apow = [_dot(ap, ap).astype(BF16) for ap in apow]
        tinv = [t + _dot(t.astype(BF16), ap) for t, ap in zip(tinv, apow)]
    for ch, t in zip(chains, tinv):
        uw = _dot(t.astype(BF16), ch['rhs'])
        ch['u'] = uw[:, :dk]
        ch['w'] = uw[:, dk:].astype(BF16)
        ch['qk'] = (_dot_nt(ch['q_b'], ch['k_b']) * ch['decay']).astype(BF16)

    def state_update(r0, groups):
        rows_of = lambda i: slice(i * c, (i + 1) * c)
        st_old = {h: st_s[h] for g in groups for h in g['heads']}
        wq = {h: _dot(jnp.concatenate([g['w'][rows_of(i)], g['q_dec'][rows_of(i)]], axis=0),
                      st_old[h].astype(BF16))
              for g in groups for i, h in enumerate(g['heads'])}
        v_new_b = [stack([g['u'][rows_of(i)] - wq[h][:c] for i, h in enumerate(g['heads'])]).astype(BF16)
                   for g in groups]
        o_all = [stack([wq[h][c:] for h in g['heads']]) + _dot(g['qk'], vb)
                 for g, vb in zip(groups, v_new_b)]
        for g, vb in zip(groups, v_new_b):
            for i, h in enumerate(g['heads']):
                st_s[h] = st_old[h] * g['g_tot'][i] + _dot_tn(g['k_dec'][rows_of(i)], vb[rows_of(i)])
        for g, o in zip(groups, o_all):
            for i, h in enumerate(g['heads']):
                zh = z_ref[0, r0:r0 + c, h * dk:(h + 1) * dk].astype(F32)
                o_ref[0, r0:r0 + c, h * dk:(h + 1) * dk] = (
                    _rms(o[rows_of(i)], nw_ref[...]) * _silu(zh)).astype(o_ref.dtype)

    for r0 in starts:
        state_update(r0, [ch for ch in chains if ch['r0'] == r0])

    xp_s[0:CONV_PAD, :] = xp_s[rows_step:rows_step + CONV_PAD, :]

    @pl.when(j == pl.num_programs(1) - 1)
    def _fin():
        sout_ref[0] = st_s[...]


def _delta(x, conv0, gb, z, s0, conv_w, norm_w, *, chunk, chunks_per_step):
    b, l, cc = x.shape
    nh, dk, dv = s0.shape[1:]
    rows_step = chunk * chunks_per_step
    assert l % rows_step == 0
    blk = lambda w: pl.BlockSpec((1, rows_step, w), lambda bi, j: (bi, j, 0))
    return pl.pallas_call(
        functools.partial(_delta_kernel, c=chunk),
        grid=(b, l // rows_step),
        in_specs=[
            blk(cc),
            pl.BlockSpec((1, CONV_PAD, cc), lambda bi, j: (bi, 0, 0)),
            blk(gb.shape[2]), blk(z.shape[2]),
            pl.BlockSpec((1, nh, dk, dv), lambda bi, j: (bi, 0, 0, 0)),
            pl.BlockSpec((CONV_W, cc), lambda bi, j: (0, 0)),
            pl.BlockSpec((1, dv), lambda bi, j: (0, 0)),
        ],
        out_specs=[blk(nh * dv), pl.BlockSpec((1, nh, dk, dv), lambda bi, j: (bi, 0, 0, 0))],
        out_shape=[jax.ShapeDtypeStruct((b, l, nh * dv), BF16), jax.ShapeDtypeStruct(s0.shape, F32)],
        scratch_shapes=[pltpu.VMEM((rows_step + CONV_PAD, cc), F32), pltpu.VMEM((nh, dk, dv), F32)],
        compiler_params=_cparams(("parallel", "arbitrary")),
        name="delta",
    )(x, conv0, gb, z, s0, conv_w, norm_w)


def _merge_ffn_kernel(h_ref, oa_ref, ob_ref, ga_ref, gb_ref, wa_ref, wb_ref, wo_ref, post_ref,
                      pre2_ref, wg_ref, wu_ref, wo2_ref, post2_ref, o_ref):
    a = _dot(oa_ref[...], wa_ref[...])
    b = _dot(ob_ref[...], wb_ref[...])
    m = jax.nn.sigmoid(ga_ref[...].astype(F32)) * a + jax.nn.sigmoid(gb_ref[...].astype(F32)) * b
    r = _dot(m.astype(BF16), wo_ref[...])
    h2 = h_ref[...] + _rms(r, post_ref[...])
    o_ref[...] = _ffn_rows(h2, pre2_ref, wg_ref, wu_ref, wo2_ref, post2_ref)


def _merge_ffn(h, oa, ob, ga, gb, wa, wb, wo, post, pre2, wg, wu, wo2, post2, *, tm):
    t, d = h.shape
    f = wg.shape[1]
    row = pl.BlockSpec((tm, d), lambda i: (i, 0))
    return pl.pallas_call(
        _merge_ffn_kernel,
        grid=(t // tm,),
        in_specs=[row, row, row, row, row, _const_spec(wa.shape), _const_spec(wb.shape), _const_spec(wo.shape),
                  _const_spec((1, d)),
                  _const_spec((1, d)), _const_spec((d, f)), _const_spec((d, f)), _const_spec((f, d)),
                  _const_spec((1, d))],
        out_specs=row,
        out_shape=jax.ShapeDtypeStruct((t, d), F32),
        compiler_params=_cparams(("parallel",)),
        name="merge_ffn",
    )(h, oa, ob, ga, gb, wa, wb, wo, post, pre2, wg, wu, wo2, post2)


def _rope_tables(pos):
    half = A_HD // 2
    inv = jnp.power(ROPE_THETA, -2.0 * jnp.arange(half, dtype=F32) / A_HD)
    ang = pos.astype(F32)[:, None] * inv[None, :]
    cos, sin = jnp.cos(ang), jnp.sin(ang)
    zero = jnp.zeros_like(sin)
    reps = HEAD_W // A_HD
    cos_t = jnp.tile(jnp.concatenate([cos, cos], axis=1), (1, reps))
    sin_lo = jnp.tile(jnp.concatenate([-sin, zero], axis=1), (1, reps))
    sin_hi = jnp.tile(jnp.concatenate([zero, sin], axis=1), (1, reps))
    return cos_t, sin_lo, sin_hi


def _pad_rows(x, rows, front=False):
    pad = rows - x.shape[1]
    cfg = [(0, 0)] * x.ndim
    cfg[1] = (pad, 0) if front else (0, pad)
    return jnp.pad(x, cfg)


def _layer(x, tabs, p, lam_init, *, tm, past):
    b, l, d = x.shape
    t = b * l
    h1 = _ffn(x.reshape(t, d), p['ffn1_pre'], p['ffn1_wg'], p['ffn1_wu'], p['ffn1_wo'], p['ffn1_post'], tm=tm)
    q, kf, kb, vf, vb, c, z, gb, ga, gbt = _proj(h1, p['mix_pre'], p['w_in'], tabs, p['alog'], p['dt'],
                                                 tm=min(tm, PROJ_TM), heads_t=past is None)
    cc = c.shape[1]
    c3 = c.reshape(b, l, cc)
    new_conv = c3[:, l - (CONV_W - 1):]
    if past is None:
        o_a = _attn_prompt(q, kb.reshape(b, l, d), vb, p['lamv'], p['subln'].reshape(-1, 1),
                           lam_init=lam_init, tq=min(ATTN_TQ, l), tk=min(ATTN_TK, l))
        conv0 = jnp.zeros((b, CONV_PAD, cc), F32)
        s0 = jnp.zeros((b, B_HEADS, B_DK, B_DK), F32)
        o_b, new_s = _delta(c3, conv0, gb.reshape(b, l, -1), z.reshape(b, l, d), s0, p['conv_w'], p['dnorm'],
                            chunk=DN_CHUNK, chunks_per_step=math.gcd(DN_CHUNKS_PER_STEP, l // DN_CHUNK))
        o_a = o_a.reshape(t, d)
        o_b = o_b.reshape(t, d)
    else:
        cache_k, cache_v, page_table, conv_state, s_state = past
        assert 2 * l == N_QROWS
        qh = q.reshape(b, l, A_HEADS, HEAD_W).transpose(0, 2, 1, 3)[:, :, None]
        lane_map = (jnp.arange(HEAD_W) // A_HD).reshape(1, 1, 1, 1, HEAD_W)
        qall = jnp.where(lane_map == jnp.arange(2).reshape(1, 1, 2, 1, 1), qh, jnp.zeros((), q.dtype))
        qall = qall.reshape(b, A_HEADS * N_QROWS, HEAD_W)
        new_rows = lambda a: _pad_rows(a.reshape(b, l, A_HEADS, HEAD_W), NEW_TOK_PAD).reshape(b, -1, HEAD_W)
        o_a = _attn_decode(qall, new_rows(kb), new_rows(vb), cache_k, cache_v, page_table, p['lamv'], p['subln'],
                           lam_init=lam_init, n_new=l)
        o_a = o_a.reshape(b, A_HEADS, N_QROWS, HEAD_W)[:, :, :l].transpose(0, 2, 1, 3).reshape(t, d)
        lp = -(-l // SAMPLE_CHUNK) * SAMPLE_CHUNK
        o_b, new_s = _delta(_pad_rows(c3, lp), _pad_rows(conv_state, CONV_PAD, front=True),
                            _pad_rows(gb.reshape(b, l, -1), lp), _pad_rows(z.reshape(b, l, d), lp),
                            s_state, p['conv_w'], p['dnorm'], chunk=SAMPLE_CHUNK, chunks_per_step=1)
        o_b = o_b[:, :l].reshape(t, d)
    y = _merge_ffn(h1, o_a, o_b, ga, gbt, p['w_a'], p['w_b'], p['w_o'], p['mix_post'],
                   p['ffn2_pre'], p['ffn2_wg'], p['ffn2_wu'], p['ffn2_wo'], p['ffn2_post'], tm=tm)
    k_rows = kf.reshape(b, l, A_HEADS, HEAD_W)
    v_rows = vf.reshape(b, l, A_HEADS, HEAD_W)
    return y.reshape(b, l, d), k_rows, v_rows, new_conv, new_s


def kernel(x_prompt, x_sample, cache_k, cache_v, state_conv, state_delta, page_table, ffn1_pre_norm, ffn1_w_in, ffn1_w_out, ffn1_post_norm, mix_pre_norm, w_in, conv_w, lambda_q1, lambda_k1, lambda_q2, lambda_k2, attn_subln, a_log, dt_bias, delta_norm, w_a_out, w_b_out, w_o, mix_post_norm, ffn2_pre_norm, ffn2_w_in, ffn2_w_out, ffn2_post_norm):
    depth = w_in.shape[0]
    d = x_prompt.shape[-1]
    lp, ls = x_prompt.shape[1], x_sample.shape[1]
    n_pages = page_table.shape[1]
    page = cache_k.shape[2]
    past_len = n_pages * page
    tabs_p = _rope_tables(jnp.arange(lp, dtype=jnp.int32))
    tabs_s1 = _rope_tables(past_len + jnp.arange(ls, dtype=jnp.int32))
    tabs_s = tuple(jnp.tile(tb, (x_sample.shape[0], 1)) for tb in tabs_s1)
    kw = B_HEADS * B_DK
    splits = np.cumsum([d, d, d, 3 * kw, kw, B_HEADS, B_HEADS, d, d])[:-1].tolist()
    yp, ys = x_prompt, x_sample
    outs = [[] for _ in range(8)]
    for i in range(depth):
        lam_init = 0.8 - 0.6 * math.exp(-0.3 * i)
        wq, wk, wv, wc, wz, wa_, wb_, wga, wgb = jnp.split(w_in[i], splits, axis=1)
        wab = jnp.pad(jnp.concatenate([wa_, wb_], axis=1), ((0, 0), (0, HEAD_W - 2 * B_HEADS)))
        row = lambda v: v.reshape(1, -1).astype(F32)
        lane_pad = lambda v: jnp.pad(v.reshape(1, -1).astype(F32), ((0, 0), (0, HEAD_W - v.shape[-1])))
        f = ffn1_w_in.shape[2] // 2
        p = {
            'ffn1_pre': row(ffn1_pre_norm[i]), 'ffn1_post': row(ffn1_post_norm[i]),
            'ffn1_wg': ffn1_w_in[i, :, :f].astype(BF16), 'ffn1_wu': ffn1_w_in[i, :, f:].astype(BF16),
            'ffn1_wo': ffn1_w_out[i].astype(BF16),
            'ffn2_pre': row(ffn2_pre_norm[i]), 'ffn2_post': row(ffn2_post_norm[i]),
            'ffn2_wg': ffn2_w_in[i, :, :f].astype(BF16), 'ffn2_wu': ffn2_w_in[i, :, f:].astype(BF16),
            'ffn2_wo': ffn2_w_out[i].astype(BF16),
            'mix_pre': row(mix_pre_norm[i]), 'mix_post': row(mix_post_norm[i]),
            'w_in': tuple(w.astype(BF16) for w in (wq, wk, wv, wc, wz, wab, wga, wgb)),
            'alog': lane_pad(a_log[i]), 'dt': lane_pad(dt_bias[i]),
            'lamv': jnp.stack([lambda_q1[i], lambda_k1[i], lambda_q2[i], lambda_k2[i]]).astype(F32),
            'subln': row(attn_subln[i]), 'dnorm': row(delta_norm[i]),
            'conv_w': conv_w[i].astype(F32),
            'w_a': w_a_out[i].astype(BF16), 'w_b': w_b_out[i].astype(BF16), 'w_o': w_o[i].astype(BF16),
        }
        yp, kp, vp, cp, sp = _layer(yp, tabs_p, p, lam_init, tm=PROMPT_TM, past=None)
        ys, ks_, vs_, cs_, ss_ = _layer(ys, tabs_s, p, lam_init, tm=x_sample.shape[0] * ls,
                                        past=(cache_k[i], cache_v[i], page_table, state_conv[i], state_delta[i]))
        for lst, val in zip(outs, (kp, vp, cp, sp, ks_, vs_, cs_, ss_)):
            lst.append(val)
    return (yp, ys) + tuple(jnp.stack(o) for o in outs)
```

```python
import functools
import math

import numpy as np
import jax
import jax.numpy as jnp
from jax import lax
from jax.experimental import pallas as pl
from jax.experimental.pallas import tpu as pltpu

F32 = jnp.float32
BF16 = jnp.bfloat16

NORM_EPS = 1e-6
ROPE_THETA = 10000.0
A_HEADS = 8
A_HD = 64
HEAD_W = 2 * A_HD
B_HEADS = 8
B_DK = 128
CONV_W = 4
DN_CHUNK = 64
NEG_BIG = -1e30
ATTN_TQ = 1024
ATTN_TK = 1024
ATTN_HEADS_PER_STEP = 2
VT_ROWS = 2 * A_HD + 16
LOG2E = math.log2(math.e)
VMEM_LIMIT_BYTES = 56 * 1024 * 1024
HIGHEST = lax.Precision.HIGHEST
PROMPT_TM = 512
PROJ_TM = 256


def _cparams(sem):
    return pltpu.CompilerParams(dimension_semantics=sem, vmem_limit_bytes=VMEM_LIMIT_BYTES)


def _const_spec(shape):
    zeros = (0,) * len(shape)
    return pl.BlockSpec(shape, lambda *_: zeros, pipeline_mode=pl.Buffered(1))


def _rms(x, w):
    return x * lax.rsqrt(jnp.mean(x * x, axis=-1, keepdims=True) + NORM_EPS) * w


def _silu(x):
    return x * jax.nn.sigmoid(x)


def _dot(a, b, precision=None):
    return jnp.dot(a, b, preferred_element_type=F32, precision=precision)


def _dot_nt(a, b, precision=None):
    return lax.dot_general(a, b, (((1,), (1,)), ((), ())), preferred_element_type=F32, precision=precision)


def _dot_tn(a, b, precision=None):
    return lax.dot_general(a, b, (((0,), (0,)), ((), ())), preferred_element_type=F32, precision=precision)


def _ffn_rows(x, pre_ref, wg_ref, wu_ref, wo_ref, post_ref):
    xn = _rms(x, pre_ref[...]).astype(BF16)
    g = _dot(xn, wg_ref[...])
    u = _dot(xn, wu_ref[...])
    a = (g * jax.nn.sigmoid(g) * u).astype(BF16)
    return x + 0.5 * _rms(_dot(a, wo_ref[...]), post_ref[...])


def _ffn_kernel(x_ref, pre_ref, wg_ref, wu_ref, wo_ref, post_ref, o_ref):
    o_ref[...] = _ffn_rows(x_ref[...], pre_ref, wg_ref, wu_ref, wo_ref, post_ref)


def _ffn(x, pre, wg, wu, wo, post, *, tm):
    t, d = x.shape
    f = wg.shape[1]
    return pl.pallas_call(
        _ffn_kernel,
        grid=(t // tm,),
        in_specs=[
            pl.BlockSpec((tm, d), lambda i: (i, 0)),
            _const_spec((1, d)), _const_spec((d, f)), _const_spec((d, f)), _const_spec((f, d)),
            _const_spec((1, d)),
        ],
        out_specs=pl.BlockSpec((tm, d), lambda i: (i, 0)),
        out_shape=jax.ShapeDtypeStruct((t, d), F32),
        compiler_params=_cparams(("parallel",)),
        name="ffn",
    )(x, pre, wg, wu, wo, post)


def _rope_heads(x, cos, sin_lo, sin_hi):
    outs = []
    for h in range(x.shape[1] // HEAD_W):
        xh = x[:, h * HEAD_W:(h + 1) * HEAD_W]
        up = pltpu.roll(xh, HEAD_W - A_HD // 2, axis=1)
        dn = pltpu.roll(xh, A_HD // 2, axis=1)
        outs.append(xh * cos + up * sin_lo + dn * sin_hi)
    return outs


def _proj_kernel(h_ref, pre_ref, wq_ref, wk_ref, wv_ref, wc_ref, wz_ref, wab_ref, wga_ref, wgb_ref,
                 cos_ref, slo_ref, shi_ref, alog_ref, dt_ref,
                 q_ref, kf_ref, kb_ref, vf_ref, vb_ref, c_ref, z_ref, gb_ref, ga_ref, gbt_ref, *, heads_t):
    u = _rms(h_ref[...], pre_ref[...]).astype(BF16)
    cos, slo, shi = cos_ref[...], slo_ref[...], shi_ref[...]
    q_scale = (A_HD ** -0.5) * LOG2E
    for h, qh in enumerate(_rope_heads(_dot(u, wq_ref[...]), cos, slo, shi)):
        if heads_t:
            q_ref[0, h] = (qh * q_scale).T.astype(BF16)
        else:
            q_ref[:, h * HEAD_W:(h + 1) * HEAD_W] = (qh * q_scale).astype(BF16)
    for h, kh in enumerate(_rope_heads(_dot(u, wk_ref[...]), cos, slo, shi)):
        kf_ref[:, h * HEAD_W:(h + 1) * HEAD_W] = kh
        kb_ref[:, h * HEAD_W:(h + 1) * HEAD_W] = kh.astype(BF16)
    v = _dot(u, wv_ref[...])
    vf_ref[...] = v
    if heads_t:
        pad_rows = VT_ROWS - HEAD_W
        ones_row = jnp.where(lax.broadcasted_iota(jnp.int32, (pad_rows, v.shape[0]), 0) == 0, 1.0, 0.0)
        for h in range(v.shape[1] // HEAD_W):
            vb_ref[0, h, 0:HEAD_W, :] = v[:, h * HEAD_W:(h + 1) * HEAD_W].T.astype(BF16)
            vb_ref[0, h, HEAD_W:VT_ROWS, :] = ones_row.astype(BF16)
    else:
        vb_ref[...] = v.astype(BF16)
    n_c = wc_ref.shape[1] // wq_ref.shape[1]
    for j in range(n_c):
        w = wq_ref.shape[1]
        c_ref[:, j * w:(j + 1) * w] = _dot(u, wc_ref[:, j * w:(j + 1) * w])
    z_ref[...] = _dot(u, wz_ref[...]).astype(BF16)
    ga_ref[...] = _dot(u, wga_ref[...]).astype(BF16)
    gbt_ref[...] = _dot(u, wgb_ref[...]).astype(BF16)
    ab = _dot(u, wab_ref[...])
    xs = ab + dt_ref[...]
    softplus = jnp.maximum(xs, 0.0) + jnp.log(1.0 + jnp.exp(-jnp.abs(xs)))
    gval = -jnp.exp(alog_ref[...]) * softplus
    lane = lax.broadcasted_iota(jnp.int32, ab.shape, 1)
    gb_ref[...] = jnp.where(lane < B_HEADS, gval, jnp.where(lane < 2 * B_HEADS, jax.nn.sigmoid(ab), 0.0))


def _proj(h, pre, ws, tabs, alog_row, dt_row, *, tm, heads_t):
    t, d = h.shape
    wq, wk, wv, wc, wz, wab, wga, wgb = ws
    cos, slo, shi = tabs
    nt = cos.shape[0] // tm
    row = lambda w: pl.BlockSpec((tm, w), lambda i: (i, 0))
    tab = pl.BlockSpec((tm, HEAD_W), lambda i: (i % nt, 0))
    sd = lambda w, dt: jax.ShapeDtypeStruct((t, w), dt)
    cw = wc.shape[1]
    if heads_t:
        nh = d // HEAD_W
        head_blk = lambda r: pl.BlockSpec((1, nh, r, tm), lambda i: (i // nt, 0, 0, i % nt))
        head_sd = lambda r: jax.ShapeDtypeStruct((t // cos.shape[0], nh, r, cos.shape[0]), BF16)
        q_spec, q_sd, v_spec, v_sd = head_blk(HEAD_W), head_sd(HEAD_W), head_blk(VT_ROWS), head_sd(VT_ROWS)
    else:
        q_spec, q_sd, v_spec, v_sd = row(d), sd(d, BF16), row(d), sd(d, BF16)
    return pl.pallas_call(
        functools.partial(_proj_kernel, heads_t=heads_t),
        grid=(t // tm,),
        in_specs=[row(d), _const_spec((1, d))] + [_const_spec(w.shape) for w in ws]
                 + [tab, tab, tab, _const_spec((1, HEAD_W)), _const_spec((1, HEAD_W))],
        out_specs=[q_spec, row(d), row(d), row(d), v_spec, row(cw), row(d), row(HEAD_W), row(d), row(d)],
        out_shape=[q_sd, sd(d, F32), sd(d, BF16), sd(d, F32), v_sd, sd(cw, F32), sd(d, BF16),
                   sd(HEAD_W, F32), sd(d, BF16), sd(d, BF16)],
        compiler_params=_cparams(("parallel",)),
        name="proj",
    )(h, pre, *ws, cos, slo, shi, alog_row, dt_row)


def _lambda_full(lam_ref, lam_init):
    l = lam_ref[...]
    s1 = jnp.sum(l[0:1] * l[1:2], axis=-1, keepdims=True)
    s2 = jnp.sum(l[2:3] * l[3:4], axis=-1, keepdims=True)
    return jnp.exp(s1) - jnp.exp(s2) + lam_init


def _attn_kernel(qi_ref, ki_ref, lam_ref, sub_ref, qt_ref, k_ref, vt_ref, o_ref, q_s, m_s, acc_s, *, lam_init):
    p = pl.program_id(2)
    ki = ki_ref[p]
    nhs, tq, tk = qt_ref.shape[1], qt_ref.shape[3], k_ref.shape[1]
    off = ki * tk - qi_ref[p] * tq

    @pl.when(ki == 0)
    def _init():
        for h in range(nhs):
            qt = qt_ref[0, h].astype(F32)
            row = lax.broadcasted_iota(jnp.int32, qt.shape, 0)
            q_s[2 * h] = jnp.where(row < A_HD, qt, 0.0).astype(BF16)
            q_s[2 * h + 1] = jnp.where(row >= A_HD, qt, 0.0).astype(BF16)
        m_s[...] = jnp.full(m_s.shape, NEG_BIG, F32)
        acc_s[...] = jnp.zeros(acc_s.shape, F32)

    krow = lax.broadcasted_iota(jnp.int32, (tk, tq), 0) + off
    qcol = lax.broadcasted_iota(jnp.int32, (tk, tq), 1)
    keep = krow <= qcol
    chains = range(2 * nhs)
    scores = [_dot(k_ref[0, :, (c // 2) * HEAD_W:(c // 2 + 1) * HEAD_W], q_s[c]) for c in chains]
    for h in range(nhs):
        probs = []
        for c in (2 * h, 2 * h + 1):
            s = jnp.where(keep, scores[c], NEG_BIG).astype(BF16)
            m_prev = m_s[c]
            m_new = jnp.maximum(m_prev, jnp.max(s, axis=0, keepdims=True).astype(F32))
            probs.append((c, jnp.exp2(m_prev - m_new), jnp.exp2(s - m_new.astype(BF16))))
            m_s[c] = m_new
        for c, alpha, pm in probs:
            acc_s[c] = alpha * acc_s[c] + _dot(vt_ref[0, h], pm)

    @pl.when(off == tq - tk)
    def _finish():
        lam = _lambda_full(lam_ref, lam_init)
        for h in range(nhs):
            n0, n1 = acc_s[2 * h], acc_s[2 * h + 1]
            ot = n0[:HEAD_W] / n0[HEAD_W:HEAD_W + 1] - lam * (n1[:HEAD_W] / n1[HEAD_W:HEAD_W + 1])
            inv = lax.rsqrt(jnp.mean(ot * ot, axis=0, keepdims=True) + NORM_EPS)
            on = ot * inv * (sub_ref[...] * (1.0 - lam_init))
            o_ref[0, :, h * HEAD_W:(h + 1) * HEAD_W] = on.T.astype(o_ref.dtype)


def _attn_prompt(qt, k, vt, lamv, subln_col, *, lam_init, tq, tk):
    b, l, d = k.shape
    assert tq % tk == 0 and l % tq == 0
    pairs = [(i, j) for i in range(l // tq) for j in range((i + 1) * (tq // tk))]
    qi_tab = jnp.asarray(np.array([pq for pq, _ in pairs], np.int32))
    ki_tab = jnp.asarray(np.array([pk for _, pk in pairs], np.int32))
    nhs = ATTN_HEADS_PER_STEP
    qt_spec = pl.BlockSpec((1, nhs, HEAD_W, tq), lambda bi, h, p, qt_, kt_: (bi, h, 0, qt_[p]))
    vt_spec = pl.BlockSpec((1, nhs, VT_ROWS, tk), lambda bi, h, p, qt_, kt_: (bi, h, 0, kt_[p]))
    k_spec = pl.BlockSpec((1, tk, nhs * HEAD_W), lambda bi, h, p, qt_, kt_: (bi, kt_[p], h))
    o_spec = pl.BlockSpec((1, tq, nhs * HEAD_W), lambda bi, h, p, qt_, kt_: (bi, qt_[p], h))
    grid_spec = pltpu.PrefetchScalarGridSpec(
        num_scalar_prefetch=2,
        grid=(b, A_HEADS // nhs, len(pairs)),
        in_specs=[
            pl.BlockSpec((4, A_HD), lambda bi, h, p, qt_, kt_: (0, 0)),
            pl.BlockSpec((HEAD_W, 1), lambda bi, h, p, qt_, kt_: (0, 0)),
            qt_spec, k_spec, vt_spec,
        ],
        out_specs=o_spec,
        scratch_shapes=[
            pltpu.VMEM((2 * nhs, HEAD_W, tq), BF16),
            pltpu.VMEM((2 * nhs, 1, tq), F32), pltpu.VMEM((2 * nhs, VT_ROWS, tq), F32),
        ],
    )
    return pl.pallas_call(
        functools.partial(_attn_kernel, lam_init=lam_init),
        grid_spec=grid_spec,
        out_shape=jax.ShapeDtypeStruct((b, l, d), BF16),
        compiler_params=_cparams(("parallel", "parallel", "arbitrary")),
        name="attn_prompt",
    )(qi_tab, ki_tab, lamv, subln_col, qt, k, vt)


N_QROWS = 8
NEW_TOK_PAD = 16


PAGES_PER_STEP = 16


def _attn_decode_kernel(pt_ref, lam_ref, sub_ref, q_ref, kn_ref, vn_ref, *rest, lam_init, n_new, n_pg):
    kc_refs, vc_refs = rest[:n_pg], rest[n_pg:2 * n_pg]
    o_ref, m_s, l_s, acc_s, bias_s = rest[2 * n_pg:]
    pg = pl.program_id(1)
    qall = q_ref[0]
    nr = qall.shape[0]

    def _head_bias(ncols, causal):
        row = lax.broadcasted_iota(jnp.int32, (nr, ncols), 0)
        col = lax.broadcasted_iota(jnp.int32, (nr, ncols), 1)
        keep = (col % A_HEADS) == (row // N_QROWS)
        if causal:
            keep = jnp.logical_and(keep, (col // A_HEADS) <= (row % n_new))
        return jnp.where(keep, 0.0, NEG_BIG).astype(F32)

    def _update(ks, vs, biases):
        ss = [_dot_nt(qall, k) + bias for k, bias in zip(ks, biases)]
        m_prev = m_s[...]
        m_new = m_prev
        for s in ss:
            m_new = jnp.maximum(m_new, jnp.max(s, axis=-1, keepdims=True))
        alpha = jnp.exp2(m_prev - m_new)
        l = alpha * l_s[...]
        acc = alpha * acc_s[...]
        for s, v in zip(ss, vs):
            pm = jnp.exp2(s - m_new)
            l = l + jnp.sum(pm, axis=-1, keepdims=True)
            acc = acc + _dot(pm.astype(BF16), v)
        l_s[...] = l
        acc_s[...] = acc
        m_s[...] = m_new

    rows = kc_refs[0].shape[1] * kc_refs[0].shape[2]

    @pl.when(pg == 0)
    def _new_rows():
        m_s[...] = jnp.full(m_s.shape, NEG_BIG, F32)
        l_s[...] = jnp.zeros(l_s.shape, F32)
        acc_s[...] = jnp.zeros(acc_s.shape, F32)
        bias_s[...] = _head_bias(rows, False)
        kn = kn_ref[0]
        _update([kn], [vn_ref[0]], [_head_bias(kn.shape[0], True)])

    flat = lambda r: r[0].reshape(rows, HEAD_W).astype(BF16)
    bias = bias_s[...]
    _update([flat(r) for r in kc_refs], [flat(r) for r in vc_refs], [bias] * n_pg)

    @pl.when(pg == pl.num_programs(1) - 1)
    def _fin():
        lam = _lambda_full(lam_ref, lam_init)
        nrm = acc_s[...] / l_s[...]
        o = nrm - lam * pltpu.roll(nrm, nr - N_QROWS // 2, axis=0)
        o_ref[0] = (_rms(o, sub_ref[...]) * (1.0 - lam_init)).astype(o_ref.dtype)


def _attn_decode(qall, kn, vn, cache_k, cache_v, page_table, lamv, subln, *, lam_init, n_new):
    bs, n_pages = page_table.shape
    page = cache_k.shape[1]
    nr = qall.shape[1]
    n_pg = math.gcd(PAGES_PER_STEP, n_pages)
    seq = lambda r: pl.BlockSpec((1, r, HEAD_W), lambda b, p, pt: (b, 0, 0))
    cache = [pl.BlockSpec((1, page, A_HEADS, HEAD_W), lambda b, p, pt, j=j: (pt[b, p * n_pg + j], 0, 0, 0))
             for j in range(n_pg)]
    grid_spec = pltpu.PrefetchScalarGridSpec(
        num_scalar_prefetch=1,
        grid=(bs, n_pages // n_pg),
        in_specs=[
            pl.BlockSpec((4, A_HD), lambda b, p, pt: (0, 0)),
            pl.BlockSpec((1, HEAD_W), lambda b, p, pt: (0, 0)),
            seq(nr), seq(kn.shape[1]), seq(vn.shape[1]),
        ] + cache + cache,
        out_specs=seq(nr),
        scratch_shapes=[pltpu.VMEM((nr, 1), F32), pltpu.VMEM((nr, 1), F32), pltpu.VMEM((nr, HEAD_W), F32),
                        pltpu.VMEM((nr, page * A_HEADS), F32)],
    )
    return pl.pallas_call(
        functools.partial(_attn_decode_kernel, lam_init=lam_init, n_new=n_new, n_pg=n_pg),
        grid_spec=grid_spec,
        out_shape=jax.ShapeDtypeStruct((bs, nr, HEAD_W), BF16),
        compiler_params=_cparams(("parallel", "arbitrary")),
        name="attn_decode",
    )(page_table, lamv, subln, qall, kn, vn, *([cache_k] * n_pg), *([cache_v] * n_pg))


CONV_PAD = 8
MXU_DEPTH = 256
SAMPLE_CHUNK = 16
DN_CHUNKS_PER_STEP = 4
SAMPLE_SEQS_PER_STEP = 4


def _delta_kernel(x_ref, conv0_ref, gb_ref, z_ref, s0_ref, cw_ref, nw_ref, o_ref, sout_ref, xp_s, st_s, *, c):
    j = pl.program_id(1)
    nb = x_ref.shape[0]
    rows_step = x_ref.shape[1]
    nh = st_s.shape[1]
    dk = st_s.shape[2]
    kw = nh * dk

    @pl.when(j == 0)
    def _init():
        xp_s[:, 0:CONV_PAD, :] = conv0_ref[...]
        st_s[...] = s0_ref[...]

    xp_s[:, CONV_PAD:CONV_PAD + rows_step, :] = x_ref[...]

    def conv(bi, r0, col):
        acc = None
        for i in range(CONV_W):
            rr = r0 + CONV_PAD - (CONV_W - 1) + i
            term = xp_s[bi, rr:rr + c, col:col + dk] * cw_ref[i:i + 1, col:col + dk]
            acc = term if acc is None else acc + term
        return _silu(acc)

    def l2n(x):
        return x * lax.rsqrt(jnp.sum(x * x, axis=-1, keepdims=True) + NORM_EPS)

    r1 = lax.broadcasted_iota(jnp.int32, (c, c), 0)
    c1 = lax.broadcasted_iota(jnp.int32, (c, c), 1)
    ltri = jnp.where(r1 >= c1, 1.0, 0.0).astype(F32)

    grp = max(1, min(nh, MXU_DEPTH // c))
    n = grp * c
    shift = int(math.log2(c))
    ri = lax.broadcasted_iota(jnp.int32, (n, n), 0)
    ci = lax.broadcasted_iota(jnp.int32, (n, n), 1)
    same = (ri >> shift) == (ci >> shift)
    incl = jnp.logical_and(same, ri >= ci)
    strict = jnp.logical_and(same, ri > ci)
    diag = ri == ci
    eye = jnp.where(diag, 1.0, 0.0).astype(F32)
    n_double = shift - 1
    stack = lambda parts: jnp.concatenate(parts, axis=0) if len(parts) > 1 else parts[0]

    starts = range(0, rows_step, c)
    chains = []
    for bi, r0 in [(bi, r0) for bi in range(nb) for r0 in starts]:
        gb = gb_ref[bi, r0:r0 + c, :]
        gc_all = _dot(ltri, gb, HIGHEST)
        for g0 in range(0, nh, grp):
            heads = range(g0, g0 + grp)
            q = stack([l2n(conv(bi, r0, h * dk)) * (dk ** -0.5) for h in heads])
            k = stack([l2n(conv(bi, r0, kw + h * dk)) for h in heads])
            v = stack([conv(bi, r0, 2 * kw + h * dk) for h in heads])
            beta = stack([gb[:, nh + h:nh + h + 1] for h in heads])
            gc = stack([gc_all[:, h:h + 1] for h in heads])
            gc_last = stack([jnp.broadcast_to(gc_all[c - 1:c, h:h + 1], (c, 1)) for h in heads])
            gc_row = jnp.sum(jnp.where(diag, gc, 0.0), axis=0, keepdims=True)
            decay = jnp.exp(jnp.where(incl, gc - gc_row, NEG_BIG))
            kbeta = k * beta
            e_gc = jnp.exp(gc)
            chains.append(dict(
                bi=bi, r0=r0, heads=heads, decay=decay, q_b=q.astype(BF16), k_b=k.astype(BF16),
                kbeta_b=kbeta.astype(BF16),
                rhs=jnp.concatenate([v * beta, kbeta * e_gc], axis=1).astype(BF16),
                q_dec=(q * e_gc).astype(BF16), k_dec=(k * jnp.exp(gc_last - gc)).astype(BF16),
                g_tot=[jnp.exp(gc_all[c - 1:c, h:h + 1]) for h in heads]))
    a_all = [jnp.where(strict, _dot_nt(ch['kbeta_b'], ch['k_b']) * ch['decay'], 0.0) for ch in chains]
    tinv = [eye - a for a in a_all]
    apow = [a.astype(BF16) for a in a_all]
    for _ in range(n_double):
        apow = [_dot(ap, ap).astype(BF16) for ap in apow]
        tinv = [t + _dot(t.astype(BF16), ap) for t, ap in zip(tinv, apow)]
    for ch, t in zip(chains, tinv):
        uw = _dot(t.astype(BF16), ch['rhs'])
        ch['u'] = uw[:, :dk]
        ch['w'] = uw[:, dk:].astype(BF16)
        ch['qk'] = (_dot_nt(ch['q_b'], ch['k_b']) * ch['decay']).astype(BF16)

    def state_update(r0, groups):
        rows_of = lambda i: slice(i * c, (i + 1) * c)
        st_old = {(g['bi'], h): st_s[g['bi'], h] for g in groups for h in g['heads']}
        wq = {(g['bi'], h): _dot(jnp.concatenate([g['w'][rows_of(i)], g['q_dec'][rows_of(i)]], axis=0),
                                 st_old[g['bi'], h].astype(BF16))
              for g in groups for i, h in enumerate(g['heads'])}
        v_new_b = [stack([g['u'][rows_of(i)] - wq[g['bi'], h][:c]
                          for i, h in enumerate(g['heads'])]).astype(BF16) for g in groups]
        o_all = [stack([wq[g['bi'], h][c:] for h in g['heads']]) + _dot(g['qk'], vb)
                 for g, vb in zip(groups, v_new_b)]
        for g, vb in zip(groups, v_new_b):
            for i, h in enumerate(g['heads']):
                st_s[g['bi'], h] = (st_old[g['bi'], h] * g['g_tot'][i]
                                    + _dot_tn(g['k_dec'][rows_of(i)], vb[rows_of(i)]))
        for g, o in zip(groups, o_all):
            for i, h in enumerate(g['heads']):
                zh = z_ref[g['bi'], r0:r0 + c, h * dk:(h + 1) * dk].astype(F32)
                o_ref[g['bi'], r0:r0 + c, h * dk:(h + 1) * dk] = (
                    _rms(o[rows_of(i)], nw_ref[...]) * _silu(zh)).astype(o_ref.dtype)

    for r0 in starts:
        state_update(r0, [ch for ch in chains if ch['r0'] == r0])

    xp_s[:, 0:CONV_PAD, :] = xp_s[:, rows_step:rows_step + CONV_PAD, :]

    @pl.when(j == pl.num_programs(1) - 1)
    def _fin():
        sout_ref[...] = st_s[...]


def _delta(x, conv0, gb, z, s0, conv_w, norm_w, *, chunk, chunks_per_step, seqs_per_step):
    b, l, cc = x.shape
    nh, dk, dv = s0.shape[1:]
    rows_step = chunk * chunks_per_step
    nb = seqs_per_step
    assert l % rows_step == 0 and b % nb == 0
    blk = lambda w: pl.BlockSpec((nb, rows_step, w), lambda bi, j: (bi, j, 0))
    return pl.pallas_call(
        functools.partial(_delta_kernel, c=chunk),
        grid=(b // nb, l // rows_step),
        in_specs=[
            blk(cc),
            pl.BlockSpec((nb, CONV_PAD, cc), lambda bi, j: (bi, 0, 0)),
            blk(gb.shape[2]), blk(z.shape[2]),
            pl.BlockSpec((nb, nh, dk, dv), lambda bi, j: (bi, 0, 0, 0)),
            pl.BlockSpec((CONV_W, cc), lambda bi, j: (0, 0)),
            pl.BlockSpec((1, dv), lambda bi, j: (0, 0)),
        ],
        out_specs=[blk(nh * dv), pl.BlockSpec((nb, nh, dk, dv), lambda bi, j: (bi, 0, 0, 0))],
        out_shape=[jax.ShapeDtypeStruct((b, l, nh * dv), BF16), jax.ShapeDtypeStruct(s0.shape, F32)],
        scratch_shapes=[pltpu.VMEM((nb, rows_step + CONV_PAD, cc), F32), pltpu.VMEM((nb, nh, dk, dv), F32)],
        compiler_params=_cparams(("parallel", "arbitrary")),
        name="delta",
    )(x, conv0, gb, z, s0, conv_w, norm_w)


def _merge_ffn_kernel(h_ref, oa_ref, ob_ref, ga_ref, gb_ref, wa_ref, wb_ref, wo_ref, post_ref,
                      pre2_ref, wg_ref, wu_ref, wo2_ref, post2_ref, o_ref):
    a = _dot(oa_ref[...], wa_ref[...])
    b = _dot(ob_ref[...], wb_ref[...])
    m = jax.nn.sigmoid(ga_ref[...].astype(F32)) * a + jax.nn.sigmoid(gb_ref[...].astype(F32)) * b
    r = _dot(m.astype(BF16), wo_ref[...])
    h2 = h_ref[...] + _rms(r, post_ref[...])
    o_ref[...] = _ffn_rows(h2, pre2_ref, wg_ref, wu_ref, wo2_ref, post2_ref)


def _merge_ffn(h, oa, ob, ga, gb, wa, wb, wo, post, pre2, wg, wu, wo2, post2, *, tm):
    t, d = h.shape
    f = wg.shape[1]
    row = pl.BlockSpec((tm, d), lambda i: (i, 0))
    return pl.pallas_call(
        _merge_ffn_kernel,
        grid=(t // tm,),
        in_specs=[row, row, row, row, row, _const_spec(wa.shape), _const_spec(wb.shape), _const_spec(wo.shape),
                  _const_spec((1, d)),
                  _const_spec((1, d)), _const_spec((d, f)), _const_spec((d, f)), _const_spec((f, d)),
                  _const_spec((1, d))],
        out_specs=row,
        out_shape=jax.ShapeDtypeStruct((t, d), F32),
        compiler_params=_cparams(("parallel",)),
        name="merge_ffn",
    )(h, oa, ob, ga, gb, wa, wb, wo, post, pre2, wg, wu, wo2, post2)


def _rope_tables(pos):
    half = A_HD // 2
    inv = jnp.power(ROPE_THETA, -2.0 * jnp.arange(half, dtype=F32) / A_HD)
    ang = pos.astype(F32)[:, None] * inv[None, :]
    cos, sin = jnp.cos(ang), jnp.sin(ang)
    zero = jnp.zeros_like(sin)
    reps = HEAD_W // A_HD
    cos_t = jnp.tile(jnp.concatenate([cos, cos], axis=1), (1, reps))
    sin_lo = jnp.tile(jnp.concatenate([-sin, zero], axis=1), (1, reps))
    sin_hi = jnp.tile(jnp.concatenate([zero, sin], axis=1), (1, reps))
    return cos_t, sin_lo, sin_hi


def _pad_rows(x, rows, front=False):
    pad = rows - x.shape[1]
    cfg = [(0, 0)] * x.ndim
    cfg[1] = (pad, 0) if front else (0, pad)
    return jnp.pad(x, cfg)


def _layer(x, tabs, p, lam_init, *, tm, past):
    b, l, d = x.shape
    t = b * l
    h1 = _ffn(x.reshape(t, d), p['ffn1_pre'], p['ffn1_wg'], p['ffn1_wu'], p['ffn1_wo'], p['ffn1_post'], tm=tm)
    q, kf, kb, vf, vb, c, z, gb, ga, gbt = _proj(h1, p['mix_pre'], p['w_in'], tabs, p['alog'], p['dt'],
                                                 tm=min(tm, PROJ_TM), heads_t=past is None)
    cc = c.shape[1]
    c3 = c.reshape(b, l, cc)
    new_conv = c3[:, l - (CONV_W - 1):]
    if past is None:
        o_a = _attn_prompt(q, kb.reshape(b, l, d), vb, p['lamv'], p['subln'].reshape(-1, 1),
                           lam_init=lam_init, tq=min(ATTN_TQ, l), tk=min(ATTN_TK, l))
        conv0 = jnp.zeros((b, CONV_PAD, cc), F32)
        s0 = jnp.zeros((b, B_HEADS, B_DK, B_DK), F32)
        o_b, new_s = _delta(c3, conv0, gb.reshape(b, l, -1), z.reshape(b, l, d), s0, p['conv_w'], p['dnorm'],
                            chunk=DN_CHUNK, chunks_per_step=math.gcd(DN_CHUNKS_PER_STEP, l // DN_CHUNK),
                            seqs_per_step=1)
        o_a = o_a.reshape(t, d)
        o_b = o_b.reshape(t, d)
    else:
        cache_k, cache_v, page_table, conv_state, s_state = past
        assert 2 * l == N_QROWS
        qh = q.reshape(b, l, A_HEADS, HEAD_W).transpose(0, 2, 1, 3)[:, :, None]
        lane_map = (jnp.arange(HEAD_W) // A_HD).reshape(1, 1, 1, 1, HEAD_W)
        qall = jnp.where(lane_map == jnp.arange(2).reshape(1, 1, 2, 1, 1), qh, jnp.zeros((), q.dtype))
        qall = qall.reshape(b, A_HEADS * N_QROWS, HEAD_W)
        new_rows = lambda a: _pad_rows(a.reshape(b, l, A_HEADS, HEAD_W), NEW_TOK_PAD).reshape(b, -1, HEAD_W)
        o_a = _attn_decode(qall, new_rows(kb), new_rows(vb), cache_k, cache_v, page_table, p['lamv'], p['subln'],
                           lam_init=lam_init, n_new=l)
        o_a = o_a.reshape(b, A_HEADS, N_QROWS, HEAD_W)[:, :, :l].transpose(0, 2, 1, 3).reshape(t, d)
        lp = -(-l // SAMPLE_CHUNK) * SAMPLE_CHUNK
        o_b, new_s = _delta(_pad_rows(c3, lp), _pad_rows(conv_state, CONV_PAD, front=True),
                            _pad_rows(gb.reshape(b, l, -1), lp), _pad_rows(z.reshape(b, l, d), lp),
                            s_state, p['conv_w'], p['dnorm'], chunk=SAMPLE_CHUNK, chunks_per_step=1,
                            seqs_per_step=math.gcd(SAMPLE_SEQS_PER_STEP, b))
        o_b = o_b[:, :l].reshape(t, d)
    y = _merge_ffn(h1, o_a, o_b, ga, gbt, p['w_a'], p['w_b'], p['w_o'], p['mix_post'],
                   p['ffn2_pre'], p['ffn2_wg'], p['ffn2_wu'], p['ffn2_wo'], p['ffn2_post'], tm=tm)
    k_rows = kf.reshape(b, l, A_HEADS, HEAD_W)
    v_rows = vf.reshape(b, l, A_HEADS, HEAD_W)
    return y.reshape(b, l, d), k_rows, v_rows, new_conv, new_s


def kernel(x_prompt, x_sample, cache_k, cache_v, state_conv, state_delta, page_table, ffn1_pre_norm, ffn1_w_in, ffn1_w_out, ffn1_post_norm, mix_pre_norm, w_in, conv_w, lambda_q1, lambda_k1, lambda_q2, lambda_k2, attn_subln, a_log, dt_bias, delta_norm, w_a_out, w_b_out, w_o, mix_post_norm, ffn2_pre_norm, ffn2_w_in, ffn2_w_out, ffn2_post_norm):
    depth = w_in.shape[0]
    d = x_prompt.shape[-1]
    lp, ls = x_prompt.shape[1], x_sample.shape[1]
    n_pages = page_table.shape[1]
    page = cache_k.shape[2]
    past_len = n_pages * page
    tabs_p = _rope_tables(jnp.arange(lp, dtype=jnp.int32))
    tabs_s1 = _rope_tables(past_len + jnp.arange(ls, dtype=jnp.int32))
    tabs_s = tuple(jnp.tile(tb, (x_sample.shape[0], 1)) for tb in tabs_s1)
    kw = B_HEADS * B_DK
    splits = np.cumsum([d, d, d, 3 * kw, kw, B_HEADS, B_HEADS, d, d])[:-1].tolist()
    yp, ys = x_prompt, x_sample
    outs = [[] for _ in range(8)]
    for i in range(depth):
        lam_init = 0.8 - 0.6 * math.exp(-0.3 * i)
        wq, wk, wv, wc, wz, wa_, wb_, wga, wgb = jnp.split(w_in[i], splits, axis=1)
        wab = jnp.pad(jnp.concatenate([wa_, wb_], axis=1), ((0, 0), (0, HEAD_W - 2 * B_HEADS)))
        row = lambda v: v.reshape(1, -1).astype(F32)
        lane_pad = lambda v: jnp.pad(v.reshape(1, -1).astype(F32), ((0, 0), (0, HEAD_W - v.shape[-1])))
        f = ffn1_w_in.shape[2] // 2
        p = {
            'ffn1_pre': row(ffn1_pre_norm[i]), 'ffn1_post': row(ffn1_post_norm[i]),
            'ffn1_wg': ffn1_w_in[i, :, :f].astype(BF16), 'ffn1_wu': ffn1_w_in[i, :, f:].astype(BF16),
            'ffn1_wo': ffn1_w_out[i].astype(BF16),
            'ffn2_pre': row(ffn2_pre_norm[i]), 'ffn2_post': row(ffn2_post_norm[i]),
            'ffn2_wg': ffn2_w_in[i, :, :f].astype(BF16), 'ffn2_wu': ffn2_w_in[i, :, f:].astype(BF16),
            'ffn2_wo': ffn2_w_out[i].astype(BF16),
            'mix_pre': row(mix_pre_norm[i]), 'mix_post': row(mix_post_norm[i]),
            'w_in': tuple(w.astype(BF16) for w in (wq, wk, wv, wc, wz, wab, wga, wgb)),
            'alog': lane_pad(a_log[i]), 'dt': lane_pad(dt_bias[i]),
            'lamv': jnp.stack([lambda_q1[i], lambda_k1[i], lambda_q2[i], lambda_k2[i]]).astype(F32),
            'subln': row(attn_subln[i]), 'dnorm': row(delta_norm[i]),
            'conv_w': conv_w[i].astype(F32),
            'w_a': w_a_out[i].astype(BF16), 'w_b': w_b_out[i].astype(BF16), 'w_o': w_o[i].astype(BF16),
        }
        yp, kp, vp, cp, sp = _layer(yp, tabs_p, p, lam_init, tm=PROMPT_TM, past=None)
        ys, ks_, vs_, cs_, ss_ = _layer(ys, tabs_s, p, lam_init, tm=x_sample.shape[0] * ls,
                                        past=(cache_k[i], cache_v[i], page_table, state_conv[i], state_delta[i]))
        for lst, val in zip(outs, (kp, vp, cp, sp, ks_, vs_, cs_, ss_)):
            lst.append(val)
    return (yp, ys) + tuple(jnp.stack(o) for o in outs)
```
